```python
import jax, jax.numpy as jnp
from jax import lax
import numpy as np

D_MODEL = 1024
BATCH = 8
SEQ = 2048
DEPTH = 4
DEC_BATCH = 128
DEC_SEQ = 8
PAST_LEN = 16384
PAGE_SIZE = 128

D_MIX = D_MODEL
D_A = D_MIX // 2
HEAD_DIM_A = 128
N_HEADS_A = D_A // HEAD_DIM_A
D_B = D_MIX - D_A
CONV_WIDTH = 31
N_META = 16
CHUNK = 64
D_IN = 4 * D_A + 3 * D_B
SPLITS = (D_A, 2 * D_A, 3 * D_A, 4 * D_A, 4 * D_A + D_B, 4 * D_A + 2 * D_B)
RMS_EPS = 1e-6
LN_EPS = 1e-5
LB_FLOOR = 1e-30

kernel_name = "hymba_style_hgrn2_conformer_conv_decode_step"


def _rmsnorm(x, g):
    xf = x.astype(jnp.float32)
    y = xf * lax.rsqrt(jnp.mean(xf * xf, axis=-1, keepdims=True) + RMS_EPS)
    return (y * g.astype(jnp.float32)).astype(x.dtype)


def _layernorm(x, g, b):
    xf = x.astype(jnp.float32)
    mu = jnp.mean(xf, axis=-1, keepdims=True)
    xc = xf - mu
    y = xc * lax.rsqrt(jnp.mean(xc * xc, axis=-1, keepdims=True) + LN_EPS)
    return (y * g.astype(jnp.float32) + b.astype(jnp.float32)).astype(x.dtype)


def _hgrn2_chunk_scan(q, k, v, logf, s0, chunk):
    B, L, H, dk = q.shape
    dv = v.shape[-1]
    n = L // chunk

    def split(t):
        return t.reshape(B, n, chunk, H, t.shape[-1]).transpose(1, 0, 3, 2, 4)

    tri = jnp.tril(jnp.ones((chunk, chunk), dtype=bool))[:, :, None]

    def step(S, inp):
        qc, kc, vc, gc = inp
        b = jnp.cumsum(gc, axis=2)
        diff = b[:, :, :, None, :] - b[:, :, None, :, :]
        decay = jnp.where(tri, jnp.exp(jnp.where(tri, diff, 0.0)), 0.0)
        scores = jnp.einsum('bhtd,bhsd,bhtsd->bhts', qc, kc, decay)
        o = (jnp.einsum('bhtd,bhdv->bhtv', qc * jnp.exp(b), S)
             + jnp.einsum('bhts,bhsv->bhtv', scores, vc))
        b_last = b[:, :, -1, :]
        S_new = (jnp.exp(b_last)[..., None] * S
                 + jnp.einsum('bhsd,bhsv->bhdv', kc * jnp.exp(b_last[:, :, None, :] - b), vc))
        return S_new, o

    S, o = lax.scan(step, s0, (split(q), split(k), split(v), split(logf)))
    return o.transpose(1, 0, 3, 2, 4).reshape(B, L, H, dv), S


def _mixer_layer(h, s0, conv_buf, segments, lb, g_pre, g_post, w_in, g_head,
                 conv_w, conv_b, ln_g, ln_b, w_pw2, b_pw2, w_out):
    B, L, _ = h.shape
    u = _rmsnorm(h, g_pre)
    z = u @ w_in
    qa, fa, ia, ga, vb, cg, gb = jnp.split(z, SPLITS, axis=-1)

    def heads(t):
        return t.reshape(B, L, N_HEADS_A, HEAD_DIM_A).astype(jnp.float32)
    zf = heads(fa)
    lbh = lb.reshape(N_HEADS_A, HEAD_DIM_A)
    logf = jnp.logaddexp(jax.nn.log_sigmoid(zf),
                         jnp.log(jnp.maximum(lbh, LB_FLOOR)) + jax.nn.log_sigmoid(-zf))
    k = (1.0 - lbh) * jax.nn.sigmoid(-zf)
    q = heads(qa)
    v = heads(ia)
    S = s0.astype(jnp.float32)
    outs = []
    start = 0
    for length, chunk in segments:
        sl = slice(start, start + length)
        o, S = _hgrn2_chunk_scan(q[:, sl], k[:, sl], v[:, sl], logf[:, sl], S, chunk)
        outs.append(o)
        start += length
    o_a = jnp.concatenate(outs, axis=1)
    o_a = _rmsnorm(o_a, g_head).reshape(B, L, D_A).astype(h.dtype) * jax.nn.silu(ga)

    a = vb * jax.nn.sigmoid(cg)
    xp = jnp.concatenate([conv_buf.astype(a.dtype), a], axis=1)
    new_buf = xp[:, -(CONV_WIDTH - 1):]
    y = lax.conv_general_dilated(
        xp, conv_w[:, None, :].astype(a.dtype), window_strides=(1,), padding='VALID',
        dimension_numbers=('NWC', 'WIO', 'NWC'), feature_group_count=D_B) + conv_b
    y = jax.nn.silu(_layernorm(y, ln_g, ln_b))
    y = (y @ w_pw2 + b_pw2) * jax.nn.silu(gb)

    out = jnp.concatenate([o_a, y], axis=-1) @ w_out
    h = h + _rmsnorm(out, g_post)
    return h, S.astype(s0.dtype), new_buf


def setup_inputs(seed: int = 0) -> dict:
    key = jax.random.key(seed)
    ks = jax.random.split(key, 20)
    f32 = jnp.float32
    nrm = lambda k, shape, s: (jax.random.normal(k, shape, f32) * s)
    return {
        "x_prompt": nrm(ks[0], (BATCH, SEQ, D_MODEL), 1.0),
        "x_sample": nrm(ks[1], (DEC_BATCH, DEC_SEQ, D_MODEL), 1.0),
        "state_hgrn": nrm(ks[2], (DEPTH, DEC_BATCH, N_HEADS_A, HEAD_DIM_A, HEAD_DIM_A), 0.1),
        "state_conv": nrm(ks[3], (DEPTH, DEC_BATCH, CONV_WIDTH - 1, D_B), 0.5),
        "meta_tokens": nrm(ks[4], (N_META, D_MODEL), 1.0),
        "norm_pre": 1.0 + nrm(ks[5], (DEPTH, D_MODEL), 0.02),
        "norm_post": 1.0 + nrm(ks[6], (DEPTH, D_MODEL), 0.02),
        "w_in": nrm(ks[7], (DEPTH, D_MODEL, D_IN), D_MODEL ** -0.5),
        "lb_logits": nrm(ks[8], (DEPTH, D_A), 0.5),
        "head_norm": 1.0 + nrm(ks[9], (DEPTH, HEAD_DIM_A), 0.02),
        "conv_w": nrm(ks[10], (DEPTH, CONV_WIDTH, D_B), CONV_WIDTH ** -0.5),
        "conv_b": nrm(ks[11], (DEPTH, D_B), 0.01),
        "conv_ln_g": 1.0 + nrm(ks[12], (DEPTH, D_B), 0.02),
        "conv_ln_b": nrm(ks[13], (DEPTH, D_B), 0.01),
        "w_pw2": nrm(ks[14], (DEPTH, D_B, D_B), D_B ** -0.5),
        "b_pw2": nrm(ks[15], (DEPTH, D_B), 0.01),
        "w_out": nrm(ks[16], (DEPTH, D_MIX, D_MODEL), D_MIX ** -0.5),
    }


def reference(x_prompt, x_sample, state_hgrn, state_conv, meta_tokens, norm_pre, norm_post,
              w_in, lb_logits, head_norm, conv_w, conv_b, conv_ln_g, conv_ln_b, w_pw2,
              b_pw2, w_out):
    p = jax.nn.softmax(lb_logits.astype(jnp.float32), axis=0)
    lb_all = jnp.cumsum(p, axis=0) - p[0:1]

    Bp, Sp, _ = x_prompt.shape
    Bs, Ls, _ = x_sample.shape
    meta = jnp.broadcast_to(meta_tokens[None].astype(x_prompt.dtype), (Bp, N_META, D_MODEL))
    hp = jnp.concatenate([meta, x_prompt], axis=1)
    hs = x_sample
    s0_p = jnp.zeros((Bp, N_HEADS_A, HEAD_DIM_A, HEAD_DIM_A), x_prompt.dtype)
    buf_p = jnp.zeros((Bp, CONV_WIDTH - 1, D_B), x_prompt.dtype)
    seg_p = ((N_META, N_META), (Sp, min(CHUNK, Sp)))
    seg_s = ((Ls, Ls),)

    sp_l, bp_l, ss_l, bs_l = [], [], [], []
    for l in range(DEPTH):
        params = (lb_all[l], norm_pre[l], norm_post[l], w_in[l], head_norm[l], conv_w[l],
                  conv_b[l], conv_ln_g[l], conv_ln_b[l], w_pw2[l], b_pw2[l], w_out[l])
        hp, sp, bp = _mixer_layer(hp, s0_p, buf_p, seg_p, *params)
        hs, ss, bs = _mixer_layer(hs, state_hgrn[l], state_conv[l], seg_s, *params)
        sp_l.append(sp); bp_l.append(bp); ss_l.append(ss); bs_l.append(bs)

    y_prompt = hp[:, N_META:]
    return (y_prompt, hs, jnp.stack(sp_l), jnp.stack(bp_l), jnp.stack(ss_l), jnp.stack(bs_l))
```

```python
import functools

import numpy as np
import jax
import jax.numpy as jnp
from jax import lax
from jax.experimental import pallas as pl
from jax.experimental.pallas import tpu as pltpu

D_MODEL = 1024
D_A = 512
HEAD_DIM = 128
N_HEADS = D_A // HEAD_DIM
D_B = 512
CONV_WIDTH = 31
TAIL = CONV_WIDTH - 1
N_META = 16
D_IN = 4 * D_A + 3 * D_B
RMS_EPS = 1e-6
LN_EPS = 1e-5
LB_FLOOR = 1e-30

XROW0 = 32
TAIL_ROW0 = XROW0 - TAIL

VMEM_LIMIT_BYTES = 56 * 1024 * 1024

_F32 = jnp.float32
_BF16 = jnp.bfloat16


def _levels(chunk):
    out = []
    m = chunk // 2
    while m >= 1:
        out.append(m)
        m //= 2
    return tuple(out)


def _decay_arg_matrix(chunk):
    blocks = []
    for m in _levels(chunk):
        a = np.zeros((chunk, chunk), np.float32)
        for r in range(chunk):
            mid = (r // (2 * m)) * 2 * m + m - 1
            if r > mid:
                a[r, mid + 1:r + 1] = 1.0
            else:
                a[r, r + 1:mid + 1] = 1.0
        blocks.append(a)
    blocks.append(np.tril(np.ones((chunk, chunk), np.float32)))
    blocks.append(np.triu(np.ones((chunk, chunk), np.float32), 1))
    return np.concatenate(blocks, axis=0)


def _sigmoid(x):
    return 1.0 / (1.0 + jnp.exp(-x))


def _dot_nt(a, b):
    return lax.dot_general(a, b, (((1,), (1,)), ((), ())), preferred_element_type=_F32)


def _dot_tn(a, b):
    return lax.dot_general(a, b, (((0,), (0,)), ((), ())), preferred_element_type=_F32)


def _layer_kernel(layer, bb, tt, chunk,
                  h_ref, s0_ref, c0_ref, amat_ref, lbl_ref, gpre_ref, gpost_ref, win_ref,
                  ghead_ref, convw_ref, convb_ref, lng_ref, lnb_ref, wpw2_ref, bpw2_ref,
                  wout_ref, ho_ref, so_ref, co_ref, st_ref, xbuf_ref, oa_ref, y_ref):
    j = pl.program_id(1)
    rows = bb * tt
    n_chunks = tt // chunk
    levels = _levels(chunk)
    nlev = len(levels)

    @pl.when(j == 0)
    def _load_state():
        for b in range(bb):
            for hh in range(N_HEADS):
                st_ref[b, hh] = s0_ref[b, hh].T
            xbuf_ref[b, pl.ds(TAIL_ROW0, TAIL), :] = c0_ref[b]

    lg = lbl_ref[...]
    pe = jnp.exp(lg - jnp.max(lg, axis=0, keepdims=True))
    p = pe / jnp.sum(pe, axis=0, keepdims=True)
    csum = p[0:1, :]
    for i in range(1, layer + 1):
        csum = csum + p[i:i + 1, :]
    lb = csum - p[0:1, :]
    lb_floor = jnp.maximum(lb, LB_FLOOR)
    one_minus_lb = 1.0 - lb

    x = h_ref[...].reshape(rows, D_MODEL)
    u = x * lax.rsqrt(jnp.mean(x * x, axis=-1, keepdims=True) + RMS_EPS) * gpre_ref[...]
    u = u.astype(_BF16)

    def zpart(i):
        return jnp.dot(u, win_ref[:, i * 512:(i + 1) * 512], preferred_element_type=_F32)

    q_all = zpart(0)
    zf_all = zpart(1)
    v_all = zpart(2)

    ti = lax.broadcasted_iota(jnp.int32, (chunk, chunk), 0)
    si = lax.broadcasted_iota(jnp.int32, (chunk, chunk), 1)
    ri = lax.broadcasted_iota(jnp.int32, (chunk, HEAD_DIM), 0)
    pair_masks = []
    second_half = []
    for m in levels:
        sh = int(np.log2(2 * m))
        same = (ti >> sh) == (si >> sh)
        pair_masks.append(same & ((ti & (2 * m - 1)) >= m) & ((si & (2 * m - 1)) < m))
        second_half.append((ri & (2 * m - 1)) >= m)
    diag_mask = ti == si
    amat = amat_ref[...]

    for b in range(bb):
        for c in range(n_chunks):
            r0 = b * tt + c * chunk
            zf = zf_all[r0:r0 + chunk]
            t = jnp.exp(-jnp.abs(zf))
            r = 1.0 / (1.0 + t)
            tr = t * r
            pos = zf >= 0
            sig = jnp.where(pos, r, tr)
            sneg = jnp.where(pos, tr, r)
            k_c = one_minus_lb * sneg
            logf = jnp.log(sig + lb_floor * sneg)
            e_all = jnp.exp(jnp.dot(amat, logf, precision=lax.Precision.HIGHEST,
                                    preferred_element_type=_F32))
            for hh in range(N_HEADS):
                hs = slice(hh * HEAD_DIM, (hh + 1) * HEAD_DIM)
                qh = q_all[r0:r0 + chunk, hs]
                kh = k_c[:, hs]
                vh = v_all[r0:r0 + chunk, hs].astype(_BF16)
                eh = e_all[:, hs]
                pm = jnp.zeros((chunk, chunk), _F32)
                for li in range(nlev):
                    em = eh[li * chunk:(li + 1) * chunk]
                    xm = (jnp.where(second_half[li], qh, kh) * em).astype(_BF16)
                    pm = jnp.where(pair_masks[li], _dot_nt(xm, xm), pm)
                pm = jnp.where(diag_mask, _dot_nt(qh.astype(_BF16), kh.astype(_BF16)), pm)
                e_q = eh[nlev * chunk:(nlev + 1) * chunk]
                e_k = eh[(nlev + 1) * chunk:(nlev + 2) * chunk]
                qt = (qh * e_q).astype(_BF16)
                kt = (kh * e_k).astype(_BF16)
                st = st_ref[b, hh]
                o = jnp.dot(pm.astype(_BF16), vh, preferred_element_type=_F32)
                o = o + _dot_nt(qt, st.astype(_BF16))
                st_ref[b, hh] = st * e_q[chunk - 1:chunk, :] + _dot_tn(vh, kt)
                oa_ref[r0:r0 + chunk, hs] = o

    ga = zpart(3)
    ghead = ghead_ref[...]
    o_parts = []
    for hh in range(N_HEADS):
        hs = slice(hh * HEAD_DIM, (hh + 1) * HEAD_DIM)
        oh = oa_ref[:, hs]
        oh = oh * lax.rsqrt(jnp.mean(oh * oh, axis=-1, keepdims=True) + RMS_EPS) * ghead
        gah = ga[:, hs]
        o_parts.append((oh * (gah * _sigmoid(gah))).astype(_BF16))
    o_a = jnp.concatenate(o_parts, axis=-1)

    a = zpart(4) * _sigmoid(zpart(5))
    convw = convw_ref[...]
    convb = convb_ref[...]
    for b in range(bb):
        xbuf_ref[b, pl.ds(XROW0, tt), :] = a[b * tt:(b + 1) * tt]
        acc = jnp.zeros((tt, D_B), _F32)
        for jj in range(CONV_WIDTH):
            acc = acc + xbuf_ref[b, pl.ds(TAIL_ROW0 + jj, tt), :] * convw[jj:jj + 1, :]
        y_ref[b * tt:(b + 1) * tt, :] = acc + convb
        new_tail = xbuf_ref[b, pl.ds(TAIL_ROW0 + tt, TAIL), :]
        xbuf_ref[b, pl.ds(TAIL_ROW0, TAIL), :] = new_tail
    y = y_ref[...]
    mu = jnp.mean(y, axis=-1, keepdims=True)
    yc = y - mu
    y = yc * lax.rsqrt(jnp.mean(yc * yc, axis=-1, keepdims=True) + LN_EPS) * lng_ref[...] + lnb_ref[...]
    y = y * _sigmoid(y)
    y = jnp.dot(y.astype(_BF16), wpw2_ref[...], preferred_element_type=_F32) + bpw2_ref[...]
    gb = zpart(6)
    y = (y * (gb * _sigmoid(gb))).astype(_BF16)

    out = (jnp.dot(o_a, wout_ref[0:D_A, :], preferred_element_type=_F32)
           + jnp.dot(y, wout_ref[D_A:D_A + D_B, :], preferred_element_type=_F32))
    out = out * lax.rsqrt(jnp.mean(out * out, axis=-1, keepdims=True) + RMS_EPS) * gpost_ref[...]
    ho_ref[...] = (x + out).reshape(bb, tt, D_MODEL)

    @pl.when(j == pl.num_programs(1) - 1)
    def _store_state():
        for b in range(bb):
            for hh in range(N_HEADS):
                so_ref[b, hh] = st_ref[b, hh].T
            co_ref[b] = xbuf_ref[b, pl.ds(TAIL_ROW0, TAIL), :]


def _mixer_layer(layer, h, s0, c0, params, *, bb, tt, chunk, broadcast_state):
    (lb_logits, norm_pre, norm_post, w_in, head_norm, conv_w, conv_b, ln_g, ln_b,
     w_pw2, b_pw2, w_out) = params
    batch, seq, _ = h.shape
    assert batch % bb == 0 and seq % tt == 0 and tt % chunk == 0
    amat = jnp.asarray(_decay_arg_matrix(chunk))
    rows = bb * tt

    def row_block(i, j):
        return (i, j, 0)

    if broadcast_state:
        state_idx4 = lambda i, j: (0, 0, 0, 0)
        state_idx3 = lambda i, j: (0, 0, 0)
    else:
        state_idx4 = lambda i, j: (i, 0, 0, 0)
        state_idx3 = lambda i, j: (i, 0, 0)

    def whole(arr):
        nd = arr.ndim
        return pl.BlockSpec(arr.shape, lambda i, j: (0,) * nd)

    def per_layer(arr):
        tail = arr.shape[1:]
        nz = len(tail)
        return pl.BlockSpec((None,) + tail, lambda i, j: (layer,) + (0,) * nz)

    in_specs = [
        pl.BlockSpec((bb, tt, D_MODEL), row_block),
        pl.BlockSpec((bb, N_HEADS, HEAD_DIM, HEAD_DIM), state_idx4),
        pl.BlockSpec((bb, TAIL, D_B), state_idx3),
        whole(amat),
        whole(lb_logits),
        per_layer(norm_pre), per_layer(norm_post), per_layer(w_in), per_layer(head_norm),
        per_layer(conv_w), per_layer(conv_b), per_layer(ln_g), per_layer(ln_b),
        per_layer(w_pw2), per_layer(b_pw2), per_layer(w_out),
    ]
    out_specs = [
        pl.BlockSpec((bb, tt, D_MODEL), row_block),
        pl.BlockSpec((bb, N_HEADS, HEAD_DIM, HEAD_DIM), lambda i, j: (i, 0, 0, 0)),
        pl.BlockSpec((bb, TAIL, D_B), lambda i, j: (i, 0, 0)),
    ]
    out_shape = [
        jax.ShapeDtypeStruct((batch, seq, D_MODEL), h.dtype),
        jax.ShapeDtypeStruct((batch, N_HEADS, HEAD_DIM, HEAD_DIM), h.dtype),
        jax.ShapeDtypeStruct((batch, TAIL, D_B), h.dtype),
    ]
    scratch = [
        pltpu.VMEM((bb, N_HEADS, HEAD_DIM, HEAD_DIM), _F32),
        pltpu.VMEM((bb, XROW0 + tt, D_B), _F32),
        pltpu.VMEM((rows, D_A), _F32),
        pltpu.VMEM((rows, D_B), _F32),
    ]
    return pl.pallas_call(
        functools.partial(_layer_kernel, layer, bb, tt, chunk),
        grid=(batch // bb, seq // tt),
        in_specs=in_specs,
        out_specs=out_specs,
        out_shape=out_shape,
        scratch_shapes=scratch,
        compiler_params=pltpu.CompilerParams(
            dimension_semantics=("arbitrary", "arbitrary"),
            vmem_limit_bytes=VMEM_LIMIT_BYTES),
    )(h, s0, c0, amat, lb_logits, norm_pre, norm_post, w_in, head_norm, conv_w, conv_b,
      ln_g, ln_b, w_pw2, b_pw2, w_out)


def kernel(x_prompt, x_sample, state_hgrn, state_conv, meta_tokens, norm_pre, norm_post, w_in,
           lb_logits, head_norm, conv_w, conv_b, conv_ln_g, conv_ln_b, w_pw2, b_pw2, w_out):
    depth = w_in.shape[0]
    row3 = lambda a: a.reshape(depth, 1, a.shape[-1])
    params = (lb_logits, row3(norm_pre), row3(norm_post), w_in.astype(_BF16), row3(head_norm),
              conv_w, row3(conv_b), row3(conv_ln_g), row3(conv_ln_b), w_pw2.astype(_BF16),
              row3(b_pw2), w_out.astype(_BF16))
    dtype = x_prompt.dtype

    hm = meta_tokens[None].astype(dtype)
    hp = x_prompt
    hs = x_sample
    zero_s = jnp.zeros((1, N_HEADS, HEAD_DIM, HEAD_DIM), dtype)
    zero_c = jnp.zeros((1, TAIL, D_B), dtype)
    sp_l, cp_l, ss_l, cs_l = [], [], [], []
    for l in range(depth):
        hm, sm, cm = _mixer_layer(l, hm, zero_s, zero_c, params, bb=1, tt=N_META, chunk=N_META,
                                  broadcast_state=False)
        hp, sp, cp = _mixer_layer(l, hp, sm, cm, params, bb=1, tt=256, chunk=64,
                                  broadcast_state=True)
        hs, ss, cs = _mixer_layer(l, hs, state_hgrn[l], state_conv[l], params, bb=16,
                                  tt=x_sample.shape[1], chunk=x_sample.shape[1],
                                  broadcast_state=False)
        sp_l.append(sp); cp_l.append(cp); ss_l.append(ss); cs_l.append(cs)
    return (hp, hs, jnp.stack(sp_l), jnp.stack(cp_l), jnp.stack(ss_l), jnp.stack(cs_l))
```

```python
import functools

import numpy as np
import jax
import jax.numpy as jnp
from jax import lax
from jax.experimental import pallas as pl
from jax.experimental.pallas import tpu as pltpu

D_MODEL = 1024
D_A = 512
HEAD_DIM = 128
N_HEADS = D_A // HEAD_DIM
D_B = 512
CONV_WIDTH = 31
TAIL = CONV_WIDTH - 1
N_META = 16
D_IN = 4 * D_A + 3 * D_B
RMS_EPS = 1e-6
LN_EPS = 1e-5
LB_FLOOR = 1e-30

SUBLANES = 8
LANES = 128
BF16_ROWS = 16
N_LANE_SLABS = D_B // LANES
XROW0 = 32
TAIL_ROW0 = XROW0 - TAIL

VMEM_LIMIT_BYTES = 56 * 1024 * 1024

_F32 = jnp.float32
_BF16 = jnp.bfloat16


def _levels(chunk):
    out = []
    m = chunk // 2
    while m >= 1:
        out.append(m)
        m //= 2
    return tuple(out)


def _decay_arg_matrix(chunk):
    blocks = []
    for m in _levels(chunk):
        a = np.zeros((chunk, chunk), np.float32)
        for r in range(chunk):
            mid = (r // (2 * m)) * 2 * m + m - 1
            if r > mid:
                a[r, mid + 1:r + 1] = 1.0
            else:
                a[r, r + 1:mid + 1] = 1.0
        blocks.append(a)
    blocks.append(np.tril(np.ones((chunk, chunk), np.float32)))
    blocks.append(np.triu(np.ones((chunk, chunk), np.float32), 1))
    return np.concatenate(blocks, axis=0)


def _sigmoid(x):
    return 1.0 / (1.0 + jnp.exp(-x))


def _dot(a, b):
    return jnp.dot(a, b, preferred_element_type=_F32)


def _dot_nt(a, b):
    return lax.dot_general(a, b, (((1,), (1,)), ((), ())), preferred_element_type=_F32)


def _dot_tn(a, b):
    return lax.dot_general(a, b, (((0,), (0,)), ((), ())), preferred_element_type=_F32)


def _split3(x):
    hi = x.astype(_BF16)
    r1 = x - hi.astype(_F32)
    mid = r1.astype(_BF16)
    lo = (r1 - mid.astype(_F32)).astype(_BF16)
    return hi, mid, lo


def _layer_kernel(layer0, n_layers, bb, tt, chunk,
                  h_ref, s0_ref, c0_ref, amat_ref, lbl_ref, gpre_ref, gpost_ref, win_ref,
                  ghead_ref, convw_ref, convb_ref, lng_ref, lnb_ref, wpw2_ref, bpw2_ref,
                  wout_ref, ho_ref, so_ref, co_ref, st_ref, xbuf_ref, oa_ref, y_ref, hall_ref):
    li = pl.program_id(0)
    gi = pl.program_id(1)
    j = pl.program_id(2)
    layer = layer0 if n_layers == 1 else layer0 + li
    rows = bb * tt
    n_chunks = tt // chunk
    levels = _levels(chunk)
    nlev = len(levels)

    @pl.when(j == 0)
    def _load_state():
        for b in range(bb):
            for hh in range(N_HEADS):
                st_ref[b, hh] = s0_ref[b, hh].T
            for cc in range(N_LANE_SLABS):
                xbuf_ref[b, cc, pl.ds(TAIL_ROW0, TAIL), :] = c0_ref[b][:, cc * LANES:(cc + 1) * LANES]

    lg = lbl_ref[...]
    pe = jnp.exp(lg - jnp.max(lg, axis=0, keepdims=True))
    p = pe / jnp.sum(pe, axis=0, keepdims=True)
    upto = lax.broadcasted_iota(jnp.int32, lg.shape, 0) <= layer
    lb = jnp.sum(jnp.where(upto, p, 0.0), axis=0, keepdims=True) - p[0:1, :]
    lb_floor = jnp.maximum(lb, LB_FLOOR)
    one_minus_lb = 1.0 - lb

    if n_layers == 1:
        x = h_ref[...].reshape(rows, D_MODEL)
    else:
        @pl.when(li == 0)
        def _first_layer_input():
            hall_ref[gi, j] = h_ref[...].reshape(rows, D_MODEL)
        x = hall_ref[gi, j]
    u = x * lax.rsqrt(jnp.mean(x * x, axis=-1, keepdims=True) + RMS_EPS) * gpre_ref[...]
    u = u.astype(_BF16)

    def zpart(i):
        return _dot(u, win_ref[:, i * 512:(i + 1) * 512])

    a = zpart(4) * _sigmoid(zpart(5))
    convw = convw_ref[...]
    convb = convb_ref[...]
    for b in range(bb):
        for cc in range(N_LANE_SLABS):
            xbuf_ref[b, cc, pl.ds(XROW0, tt), :] = a[b * tt:(b + 1) * tt, cc * LANES:(cc + 1) * LANES]

    def conv_rows(b, t0, nt):
        for cc in range(N_LANE_SLABS):
            ls = slice(cc * LANES, (cc + 1) * LANES)
            acc = None
            for tap in range(CONV_WIDTH):
                term = xbuf_ref[b, cc, pl.ds(TAIL_ROW0 + t0 + tap, nt), :] * convw[tap:tap + 1, ls]
                acc = term if acc is None else acc + term
            y_ref[b * tt + t0:b * tt + t0 + nt, ls] = acc + convb[:, ls]

    q_all = zpart(0).astype(_BF16)
    zf = zpart(1)
    v_all = zpart(2).astype(_BF16)
    t = jnp.exp(-jnp.abs(zf))
    r = 1.0 / (1.0 + t)
    tr = t * r
    pos = zf >= 0
    sig = jnp.where(pos, r, tr)
    sneg = jnp.where(pos, tr, r)
    k_all = (one_minus_lb * sneg).astype(_BF16)
    logf = jnp.log(sig + lb_floor * sneg)
    parts = _split3(logf)

    ti = lax.broadcasted_iota(jnp.int32, (chunk, chunk), 0)
    si = lax.broadcasted_iota(jnp.int32, (chunk, chunk), 1)
    ri = lax.broadcasted_iota(jnp.int32, (chunk, HEAD_DIM), 0)
    pair_masks = []
    second_half = []
    for m in levels:
        sh = int(np.log2(2 * m))
        same = (ti >> sh) == (si >> sh)
        pair_masks.append(same & ((ti & (2 * m - 1)) >= m) & ((si & (2 * m - 1)) < m))
        second_half.append((ri & (2 * m - 1)) >= m)
    diag_mask = ti == si
    amat = amat_ref[...]
    amat3 = jnp.concatenate([amat] * 3, axis=1)
    chunk_ids = [(b, c) for b in range(bb) for c in range(n_chunks)]
    heads = [slice(hh * HEAD_DIM, (hh + 1) * HEAD_DIM) for hh in range(N_HEADS)]

    e_all = {}
    for (b, c) in chunk_ids:
        rs = slice(b * tt + c * chunk, b * tt + (c + 1) * chunk)
        if chunk % BF16_ROWS == 0:
            args = _dot(amat3, jnp.concatenate([pt[rs] for pt in parts], axis=0))
        else:
            args = _dot(amat, parts[0][rs]) + _dot(amat, parts[1][rs]) + _dot(amat, parts[2][rs])
        e_all[b, c] = jnp.exp(args)

    scores = {}
    for (b, c) in chunk_ids:
        rs = slice(b * tt + c * chunk, b * tt + (c + 1) * chunk)
        for hh, hs in enumerate(heads):
            qh = q_all[rs, hs]
            kh = k_all[rs, hs]
            eh = e_all[b, c][:, hs]
            prods = []
            for lv in range(nlev):
                em = eh[lv * chunk:(lv + 1) * chunk].astype(_BF16)
                xm = jnp.where(second_half[lv], qh, kh) * em
                prods.append(_dot_nt(xm, xm))
            pm = jnp.where(diag_mask, _dot_nt(qh, kh), 0.0)
            for lv in range(nlev):
                pm = jnp.where(pair_masks[lv], prods[lv], pm)
            scores[b, c, hh] = pm.astype(_BF16)
        conv_rows(b, c * chunk, chunk)

    for b in range(bb):
        st = [st_ref[b, hh] for hh in range(N_HEADS)]
        for c in range(n_chunks):
            rs = slice(b * tt + c * chunk, b * tt + (c + 1) * chunk)
            for hh, hs in enumerate(heads):
                eh = e_all[b, c][:, hs]
                e_q = eh[nlev * chunk:(nlev + 1) * chunk]
                e_k = eh[(nlev + 1) * chunk:(nlev + 2) * chunk]
                qt = q_all[rs, hs] * e_q.astype(_BF16)
                kt = k_all[rs, hs] * e_k.astype(_BF16)
                vh = v_all[rs, hs]
                o = _dot(scores[b, c, hh], vh) + _dot_nt(qt, st[hh].astype(_BF16))
                st[hh] = st[hh] * e_q[chunk - 1:chunk, :] + _dot_tn(vh, kt)
                oa_ref[rs, hs] = o
        for hh in range(N_HEADS):
            st_ref[b, hh] = st[hh]

    for b in range(bb):
        for cc in range(N_LANE_SLABS):
            new_tail = xbuf_ref[b, cc, pl.ds(TAIL_ROW0 + tt, TAIL), :]
            xbuf_ref[b, cc, pl.ds(TAIL_ROW0, TAIL), :] = new_tail
    y = y_ref[...]
    mu = jnp.mean(y, axis=-1, keepdims=True)
    yc = y - mu
    y = yc * lax.rsqrt(jnp.mean(yc * yc, axis=-1, keepdims=True) + LN_EPS) * lng_ref[...] + lnb_ref[...]
    y = y * _sigmoid(y)
    y = _dot(y.astype(_BF16), wpw2_ref[...]) + bpw2_ref[...]
    gb = zpart(6)
    y = (y * (gb * _sigmoid(gb))).astype(_BF16)

    ga = zpart(3)
    ghead = ghead_ref[...]
    o_parts = []
    for hh, hs in enumerate(heads):
        oh = oa_ref[:, hs]
        oh = oh * lax.rsqrt(jnp.mean(oh * oh, axis=-1, keepdims=True) + RMS_EPS) * ghead
        gah = ga[:, hs]
        o_parts.append((oh * (gah * _sigmoid(gah))).astype(_BF16))
    o_a = jnp.concatenate(o_parts, axis=-1)

    out = _dot(o_a, wout_ref[0:D_A, :]) + _dot(y, wout_ref[D_A:D_A + D_B, :])
    out = out * lax.rsqrt(jnp.mean(out * out, axis=-1, keepdims=True) + RMS_EPS) * gpost_ref[...]
    h_new = x + out
    if n_layers > 1:
        hall_ref[gi, j] = h_new
    ho_ref[...] = h_new.reshape(bb, tt, D_MODEL)

    @pl.when(j == pl.num_programs(2) - 1)
    def _store_state():
        for b in range(bb):
            for hh in range(N_HEADS):
                so_ref[b, hh] = st_ref[b, hh].T
            for cc in range(N_LANE_SLABS):
                co_ref[b, :, cc * LANES:(cc + 1) * LANES] = xbuf_ref[b, cc, pl.ds(TAIL_ROW0, TAIL), :]


def _mixer_layers(layer0, n_layers, h, s0, c0, params, *, bb, tt, chunk):
    (lb_logits, norm_pre, norm_post, w_in, head_norm, conv_w, conv_b, ln_g, ln_b,
     w_pw2, b_pw2, w_out) = params
    batch, seq, _ = h.shape
    assert batch % bb == 0 and seq % tt == 0 and tt % chunk == 0
    shared_state = s0.shape[1] == 1 and batch > 1
    assert not shared_state or bb == 1
    amat = jnp.asarray(_decay_arg_matrix(chunk), _BF16)
    rows = bb * tt
    n_groups, n_tiles = batch // bb, seq // tt

    def state_idx(nd):
        if shared_state:
            return lambda l, i, j: (l, 0) + (0,) * nd
        return lambda l, i, j: (l, i) + (0,) * nd

    def whole(arr):
        nd = arr.ndim
        return pl.BlockSpec(arr.shape, lambda l, i, j: (0,) * nd)

    def per_layer(arr):
        tail = arr.shape[1:]
        nz = len(tail)
        return pl.BlockSpec((None,) + tail, lambda l, i, j: (layer0 + l,) + (0,) * nz)

    state_block = (None, bb, N_HEADS, HEAD_DIM, HEAD_DIM)
    tail_block = (None, bb, TAIL, D_B)
    in_specs = [
        pl.BlockSpec((bb, tt, D_MODEL), lambda l, i, j: (i, j, 0)),
        pl.BlockSpec(state_block, state_idx(3)),
        pl.BlockSpec(tail_block, state_idx(2)),
        whole(amat),
        whole(lb_logits),
        per_layer(norm_pre), per_layer(norm_post), per_layer(w_in), per_layer(head_norm),
        per_layer(conv_w), per_layer(conv_b), per_layer(ln_g), per_layer(ln_b),
        per_layer(w_pw2), per_layer(b_pw2), per_layer(w_out),
    ]
    last = n_layers - 1

    def h_out_idx(l, i, j):
        if n_layers == 1:
            return (i, j, 0)
        return (jnp.where(l == last, i, 0), jnp.where(l == last, j, 0), 0)

    out_specs = [
        pl.BlockSpec((bb, tt, D_MODEL), h_out_idx),
        pl.BlockSpec(state_block, lambda l, i, j: (l, i, 0, 0, 0)),
        pl.BlockSpec(tail_block, lambda l, i, j: (l, i, 0, 0)),
    ]
    out_shape = [
        jax.ShapeDtypeStruct((batch, seq, D_MODEL), h.dtype),
        jax.ShapeDtypeStruct((n_layers, batch, N_HEADS, HEAD_DIM, HEAD_DIM), h.dtype),
        jax.ShapeDtypeStruct((n_layers, batch, TAIL, D_B), h.dtype),
    ]
    hall_shape = (n_groups, n_tiles, rows, D_MODEL) if n_layers > 1 else (1, 1, SUBLANES, 128)
    scratch = [
        pltpu.VMEM((bb, N_HEADS, HEAD_DIM, HEAD_DIM), _F32),
        pltpu.VMEM((bb, N_LANE_SLABS, XROW0 + tt, LANES), _F32),
        pltpu.VMEM((rows, D_A), _F32),
        pltpu.VMEM((rows, D_B), _F32),
        pltpu.VMEM(hall_shape, _F32),
    ]
    return pl.pallas_call(
        functools.partial(_layer_kernel, layer0, n_layers, bb, tt, chunk),
        grid=(n_layers, n_groups, n_tiles),
        in_specs=in_specs,
        out_specs=out_specs,
        out_shape=out_shape,
        scratch_shapes=scratch,
        compiler_params=pltpu.CompilerParams(
            dimension_semantics=("arbitrary", "arbitrary", "arbitrary"),
            vmem_limit_bytes=VMEM_LIMIT_BYTES),
    )(h, s0, c0, amat, lb_logits, norm_pre, norm_post, w_in, head_norm, conv_w, conv_b,
      ln_g, ln_b, w_pw2, b_pw2, w_out)


def kernel(x_prompt, x_sample, state_hgrn, state_conv, meta_tokens, norm_pre, norm_post, w_in,
           lb_logits, head_norm, conv_w, conv_b, conv_ln_g, conv_ln_b, w_pw2, b_pw2, w_out):
    depth = w_in.shape[0]
    row3 = lambda a: a.reshape(depth, 1, a.shape[-1])
    params = (lb_logits, row3(norm_pre), row3(norm_post), w_in.astype(_BF16), row3(head_norm),
              conv_w, row3(conv_b), row3(conv_ln_g), row3(conv_ln_b), w_pw2.astype(_BF16),
              row3(b_pw2), w_out.astype(_BF16))
    dtype = x_prompt.dtype

    n_dec = x_sample.shape[1]
    y_sample, s_sample, c_sample = _mixer_layers(
        0, depth, x_sample, state_hgrn, state_conv, params, bb=16, tt=n_dec, chunk=n_dec)

    hm = meta_tokens[None].astype(dtype)
    hp = x_prompt
    zero_s = jnp.zeros((1, 1, N_HEADS, HEAD_DIM, HEAD_DIM), dtype)
    zero_c = jnp.zeros((1, 1, TAIL, D_B), dtype)
    sp_l, cp_l = [], []
    for l in range(depth):
        hm, sm, cm = _mixer_layers(l, 1, hm, zero_s, zero_c, params, bb=1, tt=N_META, chunk=N_META)
        hp, sp, cp = _mixer_layers(l, 1, hp, sm, cm, params, bb=1, tt=256, chunk=64)
        sp_l.append(sp)
        cp_l.append(cp)
    return (hp, y_sample, jnp.concatenate(sp_l), jnp.concatenate(cp_l), s_sample, c_sample)
```

```python
import functools

import numpy as np
import jax
import jax.numpy as jnp
from jax import lax
from jax.experimental import pallas as pl
from jax.experimental.pallas import tpu as pltpu

D_MODEL = 1024
D_A = 512
HEAD_DIM = 128
N_HEADS = D_A // HEAD_DIM
D_B = 512
CONV_WIDTH = 31
TAIL = CONV_WIDTH - 1
N_META = 16
D_IN = 4 * D_A + 3 * D_B
RMS_EPS = 1e-6
LN_EPS = 1e-5
LB_FLOOR = 1e-30

SUBLANES = 8
LANES = 128
BF16_ROWS = 16
N_LANE_SLABS = D_B // LANES
XROW0 = 32
TAIL_ROW0 = XROW0 - TAIL

VMEM_LIMIT_BYTES = 56 * 1024 * 1024
PROMPT_TILE = 512
PROMPT_CHUNK = 64

_F32 = jnp.float32
_BF16 = jnp.bfloat16


def _levels(chunk):
    out = []
    m = chunk // 2
    while m >= 1:
        out.append(m)
        m //= 2
    return tuple(out)


def _decay_arg_matrix(chunk):
    blocks = []
    for m in _levels(chunk):
        a = np.zeros((chunk, chunk), np.float32)
        for r in range(chunk):
            mid = (r // (2 * m)) * 2 * m + m - 1
            if r > mid:
                a[r, mid + 1:r + 1] = 1.0
            else:
                a[r, r + 1:mid + 1] = 1.0
        blocks.append(a)
    blocks.append(np.tril(np.ones((chunk, chunk), np.float32)))
    blocks.append(np.triu(np.ones((chunk, chunk), np.float32), 1))
    return np.concatenate(blocks, axis=0)


def _sigmoid(x):
    return 1.0 / (1.0 + jnp.exp(-x))


def _dot(a, b):
    return jnp.dot(a, b, preferred_element_type=_F32)


def _dot_nt(a, b):
    return lax.dot_general(a, b, (((1,), (1,)), ((), ())), preferred_element_type=_F32)


def _dot_tn(a, b):
    return lax.dot_general(a, b, (((0,), (0,)), ((), ())), preferred_element_type=_F32)


def _split3(x):
    hi = x.astype(_BF16)
    r1 = x - hi.astype(_F32)
    mid = r1.astype(_BF16)
    lo = (r1 - mid.astype(_F32)).astype(_BF16)
    return hi, mid, lo


def _layer_kernel(layer0, n_layers, bb, tt, chunk, n_tiles,
                  h_ref, s0_ref, c0_ref, amat_ref, lbl_ref, gpre_ref, gpost_ref, win_ref,
                  ghead_ref, convw_ref, convb_ref, lng_ref, lnb_ref, wpw2_ref, bpw2_ref,
                  wout_ref, ho_ref, so_ref, co_ref, st_ref, xbuf_ref, oa_ref, y_ref, hall_ref):
    li = pl.program_id(0)
    gi = pl.program_id(1)
    j = pl.program_id(2)
    layer = layer0 if n_layers == 1 else layer0 + li
    rows = bb * tt
    n_chunks = tt // chunk
    levels = _levels(chunk)
    nlev = len(levels)

    single_tile = n_tiles == 1

    def load_state():
        for b in range(bb):
            for hh in range(N_HEADS):
                st_ref[b, hh] = s0_ref[b, hh].T
            for cc in range(N_LANE_SLABS):
                xbuf_ref[b, cc, pl.ds(TAIL_ROW0, TAIL), :] = c0_ref[b][:, cc * LANES:(cc + 1) * LANES]

    def store_state():
        for b in range(bb):
            for hh in range(N_HEADS):
                so_ref[b, hh] = st_ref[b, hh].T
            for cc in range(N_LANE_SLABS):
                co_ref[b, :, cc * LANES:(cc + 1) * LANES] = xbuf_ref[b, cc, pl.ds(TAIL_ROW0, TAIL), :]

    if not single_tile:
        pl.when(j == 0)(load_state)

    lg = lbl_ref[...]
    pe = jnp.exp(lg - jnp.max(lg, axis=0, keepdims=True))
    p = pe / jnp.sum(pe, axis=0, keepdims=True)
    upto = lax.broadcasted_iota(jnp.int32, lg.shape, 0) <= layer
    lb = jnp.sum(jnp.where(upto, p, 0.0), axis=0, keepdims=True) - p[0:1, :]
    lb_floor = jnp.maximum(lb, LB_FLOOR)
    one_minus_lb = 1.0 - lb

    if n_layers == 1:
        x = h_ref[...].reshape(rows, D_MODEL)
    else:
        @pl.when(li == 0)
        def _first_layer_input():
            hall_ref[gi, j] = h_ref[...].reshape(rows, D_MODEL)
        x = hall_ref[gi, j]
    if single_tile:
        load_state()
    u = x * lax.rsqrt(jnp.mean(x * x, axis=-1, keepdims=True) + RMS_EPS) * gpre_ref[...]
    u = u.astype(_BF16)

    def zpart(i):
        return _dot(u, win_ref[:, i * 512:(i + 1) * 512])

    a = zpart(4) * _sigmoid(zpart(5))
    convw = convw_ref[...]
    convb = convb_ref[...]
    for b in range(bb):
        for cc in range(N_LANE_SLABS):
            xbuf_ref[b, cc, pl.ds(XROW0, tt), :] = a[b * tt:(b + 1) * tt, cc * LANES:(cc + 1) * LANES]

    def conv_rows(b, t0, nt):
        for cc in range(N_LANE_SLABS):
            ls = slice(cc * LANES, (cc + 1) * LANES)
            acc = None
            for tap in range(CONV_WIDTH):
                term = xbuf_ref[b, cc, pl.ds(TAIL_ROW0 + t0 + tap, nt), :] * convw[tap:tap + 1, ls]
                acc = term if acc is None else acc + term
            y_ref[b * tt + t0:b * tt + t0 + nt, ls] = acc + convb[:, ls]

    q_all = zpart(0).astype(_BF16)
    zf = zpart(1)
    v_all = zpart(2).astype(_BF16)
    t = jnp.exp(-jnp.abs(zf))
    r = 1.0 / (1.0 + t)
    tr = t * r
    pos = zf >= 0
    sig = jnp.where(pos, r, tr)
    sneg = jnp.where(pos, tr, r)
    k_all = (one_minus_lb * sneg).astype(_BF16)
    logf = jnp.log(sig + lb_floor * sneg)
    parts = _split3(logf)

    ti = lax.broadcasted_iota(jnp.int32, (chunk, chunk), 0)
    si = lax.broadcasted_iota(jnp.int32, (chunk, chunk), 1)
    ri = lax.broadcasted_iota(jnp.int32, (chunk, HEAD_DIM), 0)
    pair_masks = []
    second_half = []
    for m in levels:
        sh = int(np.log2(2 * m))
        same = (ti >> sh) == (si >> sh)
        pair_masks.append(same & ((ti & (2 * m - 1)) >= m) & ((si & (2 * m - 1)) < m))
        second_half.append((ri & (2 * m - 1)) >= m)
    diag_mask = ti == si
    amat = amat_ref[...]
    amat3 = jnp.concatenate([amat] * 3, axis=1)
    chunk_ids = [(b, c) for b in range(bb) for c in range(n_chunks)]
    heads = [slice(hh * HEAD_DIM, (hh + 1) * HEAD_DIM) for hh in range(N_HEADS)]

    e_all = {}
    for (b, c) in chunk_ids:
        rs = slice(b * tt + c * chunk, b * tt + (c + 1) * chunk)
        if chunk % BF16_ROWS == 0:
            args = _dot(amat3, jnp.concatenate([pt[rs] for pt in parts], axis=0))
        else:
            args = _dot(amat, parts[0][rs]) + _dot(amat, parts[1][rs]) + _dot(amat, parts[2][rs])
        e_all[b, c] = jnp.exp(args)

    scores = {}
    for (b, c) in chunk_ids:
        rs = slice(b * tt + c * chunk, b * tt + (c + 1) * chunk)
        for hh, hs in enumerate(heads):
            qh = q_all[rs, hs]
            kh = k_all[rs, hs]
            eh = e_all[b, c][:, hs]
            prods = []
            for lv in range(nlev):
                em = eh[lv * chunk:(lv + 1) * chunk].astype(_BF16)
                xm = jnp.where(second_half[lv], qh, kh) * em
                prods.append(_dot_nt(xm, xm))
            pm = jnp.where(diag_mask, _dot_nt(qh, kh), 0.0)
            for lv in range(nlev):
                pm = jnp.where(pair_masks[lv], prods[lv], pm)
            scores[b, c, hh] = pm.astype(_BF16)
        conv_rows(b, c * chunk, chunk)

    for b in range(bb):
        st = [st_ref[b, hh] for hh in range(N_HEADS)]
        for c in range(n_chunks):
            rs = slice(b * tt + c * chunk, b * tt + (c + 1) * chunk)
            for hh, hs in enumerate(heads):
                eh = e_all[b, c][:, hs]
                e_q = eh[nlev * chunk:(nlev + 1) * chunk]
                e_k = eh[(nlev + 1) * chunk:(nlev + 2) * chunk]
                qt = q_all[rs, hs] * e_q.astype(_BF16)
                kt = k_all[rs, hs] * e_k.astype(_BF16)
                vh = v_all[rs, hs]
                o = _dot(scores[b, c, hh], vh) + _dot_nt(qt, st[hh].astype(_BF16))
                st[hh] = st[hh] * e_q[chunk - 1:chunk, :] + _dot_tn(vh, kt)
                oa_ref[rs, hs] = o
        for hh in range(N_HEADS):
            st_ref[b, hh] = st[hh]

    for b in range(bb):
        for cc in range(N_LANE_SLABS):
            new_tail = xbuf_ref[b, cc, pl.ds(TAIL_ROW0 + tt, TAIL), :]
            xbuf_ref[b, cc, pl.ds(TAIL_ROW0, TAIL), :] = new_tail
    y = y_ref[...]
    mu = jnp.mean(y, axis=-1, keepdims=True)
    yc = y - mu
    y = yc * lax.rsqrt(jnp.mean(yc * yc, axis=-1, keepdims=True) + LN_EPS) * lng_ref[...] + lnb_ref[...]
    y = y * _sigmoid(y)
    y = _dot(y.astype(_BF16), wpw2_ref[...]) + bpw2_ref[...]
    gb = zpart(6)
    y = (y * (gb * _sigmoid(gb))).astype(_BF16)

    ga = zpart(3)
    ghead = ghead_ref[...]
    o_parts = []
    for hh, hs in enumerate(heads):
        oh = oa_ref[:, hs]
        oh = oh * lax.rsqrt(jnp.mean(oh * oh, axis=-1, keepdims=True) + RMS_EPS) * ghead
        gah = ga[:, hs]
        o_parts.append((oh * (gah * _sigmoid(gah))).astype(_BF16))
    o_a = jnp.concatenate(o_parts, axis=-1)

    out = _dot(o_a, wout_ref[0:D_A, :]) + _dot(y, wout_ref[D_A:D_A + D_B, :])
    out = out * lax.rsqrt(jnp.mean(out * out, axis=-1, keepdims=True) + RMS_EPS) * gpost_ref[...]
    h_new = x + out
    if n_layers > 1:
        hall_ref[gi, j] = h_new
    ho_ref[...] = h_new.reshape(bb, tt, D_MODEL)

    if single_tile:
        store_state()
    else:
        pl.when(j == n_tiles - 1)(store_state)


def _mixer_layers(layer0, n_layers, h, s0, c0, params, *, bb, tt, chunk, state_layer0=0):
    (lb_logits, norm_pre, norm_post, w_in, head_norm, conv_w, conv_b, ln_g, ln_b,
     w_pw2, b_pw2, w_out) = params
    batch, seq, _ = h.shape
    assert batch % bb == 0 and seq % tt == 0 and tt % chunk == 0
    shared_state = s0.shape[1] == 1 and batch > 1
    assert not shared_state or bb == 1
    amat = jnp.asarray(_decay_arg_matrix(chunk), _BF16)
    rows = bb * tt
    n_groups, n_tiles = batch // bb, seq // tt

    def state_idx(nd):
        if shared_state:
            return lambda l, i, j: (state_layer0 + l, 0) + (0,) * nd
        return lambda l, i, j: (state_layer0 + l, i) + (0,) * nd

    def whole(arr):
        nd = arr.ndim
        return pl.BlockSpec(arr.shape, lambda l, i, j: (0,) * nd)

    def per_layer(arr):
        tail = arr.shape[1:]
        nz = len(tail)
        return pl.BlockSpec((None,) + tail, lambda l, i, j: (layer0 + l,) + (0,) * nz)

    state_block = (None, bb, N_HEADS, HEAD_DIM, HEAD_DIM)
    tail_block = (None, bb, TAIL, D_B)
    in_specs = [
        pl.BlockSpec((bb, tt, D_MODEL), lambda l, i, j: (i, j, 0)),
        pl.BlockSpec(state_block, state_idx(3)),
        pl.BlockSpec(tail_block, state_idx(2)),
        whole(amat),
        whole(lb_logits),
        per_layer(norm_pre), per_layer(norm_post), per_layer(w_in), per_layer(head_norm),
        per_layer(conv_w), per_layer(conv_b), per_layer(ln_g), per_layer(ln_b),
        per_layer(w_pw2), per_layer(b_pw2), per_layer(w_out),
    ]
    last = n_layers - 1

    def h_out_idx(l, i, j):
        if n_layers == 1:
            return (i, j, 0)
        return (jnp.where(l == last, i, 0), jnp.where(l == last, j, 0), 0)

    out_specs = [
        pl.BlockSpec((bb, tt, D_MODEL), h_out_idx),
        pl.BlockSpec(state_block, lambda l, i, j: (l, i, 0, 0, 0)),
        pl.BlockSpec(tail_block, lambda l, i, j: (l, i, 0, 0)),
    ]
    out_shape = [
        jax.ShapeDtypeStruct((batch, seq, D_MODEL), h.dtype),
        jax.ShapeDtypeStruct((n_layers, batch, N_HEADS, HEAD_DIM, HEAD_DIM), h.dtype),
        jax.ShapeDtypeStruct((n_layers, batch, TAIL, D_B), h.dtype),
    ]
    hall_shape = (n_groups, n_tiles, rows, D_MODEL) if n_layers > 1 else (1, 1, SUBLANES, 128)
    scratch = [
        pltpu.VMEM((bb, N_HEADS, HEAD_DIM, HEAD_DIM), _F32),
        pltpu.VMEM((bb, N_LANE_SLABS, XROW0 + tt, LANES), _F32),
        pltpu.VMEM((rows, D_A), _F32),
        pltpu.VMEM((rows, D_B), _F32),
        pltpu.VMEM(hall_shape, _F32),
    ]
    return pl.pallas_call(
        functools.partial(_layer_kernel, layer0, n_layers, bb, tt, chunk, n_tiles),
        grid=(n_layers, n_groups, n_tiles),
        in_specs=in_specs,
        out_specs=out_specs,
        out_shape=out_shape,
        scratch_shapes=scratch,
        compiler_params=pltpu.CompilerParams(
            dimension_semantics=("arbitrary", "arbitrary", "arbitrary"),
            vmem_limit_bytes=VMEM_LIMIT_BYTES),
    )(h, s0, c0, amat, lb_logits, norm_pre, norm_post, w_in, head_norm, conv_w, conv_b,
      ln_g, ln_b, w_pw2, b_pw2, w_out)


def kernel(x_prompt, x_sample, state_hgrn, state_conv, meta_tokens, norm_pre, norm_post, w_in,
           lb_logits, head_norm, conv_w, conv_b, conv_ln_g, conv_ln_b, w_pw2, b_pw2, w_out):
    depth = w_in.shape[0]
    row3 = lambda a: a.reshape(depth, 1, a.shape[-1])
    params = (lb_logits, row3(norm_pre), row3(norm_post), w_in.astype(_BF16), row3(head_norm),
              conv_w, row3(conv_b), row3(conv_ln_g), row3(conv_ln_b), w_pw2.astype(_BF16),
              row3(b_pw2), w_out.astype(_BF16))
    dtype = x_prompt.dtype

    n_dec = x_sample.shape[1]
    y_sample, s_sample, c_sample = _mixer_layers(
        0, depth, x_sample, state_hgrn, state_conv, params, bb=16, tt=n_dec, chunk=n_dec)

    hm = meta_tokens[None].astype(dtype)
    zero_s = jnp.zeros((depth, 1, N_HEADS, HEAD_DIM, HEAD_DIM), dtype)
    zero_c = jnp.zeros((depth, 1, TAIL, D_B), dtype)
    _, s_meta, c_meta = _mixer_layers(0, depth, hm, zero_s, zero_c, params, bb=1, tt=N_META,
                                      chunk=N_META)
    hp = x_prompt
    sp_l, cp_l = [], []
    for l in range(depth):
        hp, sp, cp = _mixer_layers(l, 1, hp, s_meta, c_meta, params, bb=1, tt=PROMPT_TILE,
                                   chunk=PROMPT_CHUNK, state_layer0=l)
        sp_l.append(sp)
        cp_l.append(cp)
    return (hp, y_sample, jnp.concatenate(sp_l), jnp.concatenate(cp_l), s_sample, c_sample)
```

```python
import functools

import numpy as np
import jax
import jax.numpy as jnp
from jax import lax
from jax.experimental import pallas as pl
from jax.experimental.pallas import tpu as pltpu

D_MODEL = 1024
D_A = 512
HEAD_DIM = 128
N_HEADS = D_A // HEAD_DIM
D_B = 512
CONV_WIDTH = 31
TAIL = CONV_WIDTH - 1
N_META = 16
D_IN = 4 * D_A + 3 * D_B
RMS_EPS = 1e-6
LN_EPS = 1e-5
LB_FLOOR = 1e-30

SUBLANES = 8
LANES = 128
BF16_ROWS = 16
N_LANE_SLABS = D_B // LANES
XROW0 = 32
TAIL_ROW0 = XROW0 - TAIL

VMEM_LIMIT_BYTES = 56 * 1024 * 1024
PROMPT_TILE = 1024
PROMPT_CHUNK = 64

_F32 = jnp.float32
_BF16 = jnp.bfloat16


def _levels(chunk):
    out = []
    m = chunk // 2
    while m >= 1:
        out.append(m)
        m //= 2
    return tuple(out)


def _decay_arg_matrix(chunk):
    blocks = []
    for m in _levels(chunk):
        a = np.zeros((chunk, chunk), np.float32)
        for r in range(chunk):
            mid = (r // (2 * m)) * 2 * m + m - 1
            if r > mid:
                a[r, mid + 1:r + 1] = 1.0
            else:
                a[r, r + 1:mid + 1] = 1.0
        blocks.append(a)
    blocks.append(np.tril(np.ones((chunk, chunk), np.float32)))
    blocks.append(np.triu(np.ones((chunk, chunk), np.float32), 1))
    return np.concatenate(blocks, axis=0)


def _sigmoid(x):
    return 1.0 / (1.0 + jnp.exp(-x))


def _dot(a, b):
    return jnp.dot(a, b, preferred_element_type=_F32)


def _dot_nt(a, b):
    return lax.dot_general(a, b, (((1,), (1,)), ((), ())), preferred_element_type=_F32)


def _dot_tn(a, b):
    return lax.dot_general(a, b, (((0,), (0,)), ((), ())), preferred_element_type=_F32)


def _split3(x):
    hi = x.astype(_BF16)
    r1 = x - hi.astype(_F32)
    mid = r1.astype(_BF16)
    lo = (r1 - mid.astype(_F32)).astype(_BF16)
    return hi, mid, lo


def _layer_kernel(layer0, n_layers, bb, tt, chunk, n_tiles,
                  h_ref, s0_ref, c0_ref, amat_ref, lbl_ref, gpre_ref, gpost_ref, win_ref,
                  ghead_ref, convw_ref, convb_ref, lng_ref, lnb_ref, wpw2_ref, bpw2_ref,
                  wout_ref, ho_ref, so_ref, co_ref, st_ref, xbuf_ref, oa_ref, y_ref, hall_ref):
    li = pl.program_id(0)
    gi = pl.program_id(1)
    j = pl.program_id(2)
    layer = layer0 if n_layers == 1 else layer0 + li
    rows = bb * tt
    n_chunks = tt // chunk
    levels = _levels(chunk)
    nlev = len(levels)

    single_tile = n_tiles == 1

    def load_state():
        for b in range(bb):
            for hh in range(N_HEADS):
                st_ref[b, hh] = s0_ref[b, hh].T
            for cc in range(N_LANE_SLABS):
                xbuf_ref[b, cc, pl.ds(TAIL_ROW0, TAIL), :] = c0_ref[b][:, cc * LANES:(cc + 1) * LANES]

    def store_state():
        for b in range(bb):
            for hh in range(N_HEADS):
                so_ref[b, hh] = st_ref[b, hh].T
            for cc in range(N_LANE_SLABS):
                co_ref[b, :, cc * LANES:(cc + 1) * LANES] = xbuf_ref[b, cc, pl.ds(TAIL_ROW0, TAIL), :]

    if not single_tile:
        pl.when(j == 0)(load_state)

    lg = lbl_ref[...]
    pe = jnp.exp(lg - jnp.max(lg, axis=0, keepdims=True))
    p = pe / jnp.sum(pe, axis=0, keepdims=True)
    upto = lax.broadcasted_iota(jnp.int32, lg.shape, 0) <= layer
    lb = jnp.sum(jnp.where(upto, p, 0.0), axis=0, keepdims=True) - p[0:1, :]
    lb_floor = jnp.maximum(lb, LB_FLOOR)
    one_minus_lb = 1.0 - lb

    if n_layers == 1:
        x = h_ref[...].reshape(rows, D_MODEL)
    else:
        @pl.when(li == 0)
        def _first_layer_input():
            hall_ref[gi, j] = h_ref[...].reshape(rows, D_MODEL)
        x = hall_ref[gi, j]
    if single_tile:
        load_state()
    u = x * lax.rsqrt(jnp.mean(x * x, axis=-1, keepdims=True) + RMS_EPS) * gpre_ref[...]
    u = u.astype(_BF16)

    def zpart(i):
        return _dot(u, win_ref[:, i * 512:(i + 1) * 512])

    a = zpart(4) * _sigmoid(zpart(5))
    convw = convw_ref[...]
    convb = convb_ref[...]
    for b in range(bb):
        for cc in range(N_LANE_SLABS):
            xbuf_ref[b, cc, pl.ds(XROW0, tt), :] = a[b * tt:(b + 1) * tt, cc * LANES:(cc + 1) * LANES]

    def conv_rows(b, t0, nt):
        for cc in range(N_LANE_SLABS):
            ls = slice(cc * LANES, (cc + 1) * LANES)
            acc = None
            for tap in range(CONV_WIDTH):
                term = xbuf_ref[b, cc, pl.ds(TAIL_ROW0 + t0 + tap, nt), :] * convw[tap:tap + 1, ls]
                acc = term if acc is None else acc + term
            y_ref[b * tt + t0:b * tt + t0 + nt, ls] = acc + convb[:, ls]

    q_all = zpart(0).astype(_BF16)
    zf = zpart(1)
    v_all = zpart(2).astype(_BF16)
    t = jnp.exp(-jnp.abs(zf))
    r = 1.0 / (1.0 + t)
    tr = t * r
    pos = zf >= 0
    sig = jnp.where(pos, r, tr)
    sneg = jnp.where(pos, tr, r)
    k_all = (one_minus_lb * sneg).astype(_BF16)
    logf = jnp.log(sig + lb_floor * sneg)
    parts = _split3(logf)

    ti = lax.broadcasted_iota(jnp.int32, (chunk, chunk), 0)
    si = lax.broadcasted_iota(jnp.int32, (chunk, chunk), 1)
    ri = lax.broadcasted_iota(jnp.int32, (chunk, HEAD_DIM), 0)
    pair_masks = []
    second_half = []
    for m in levels:
        sh = int(np.log2(2 * m))
        same = (ti >> sh) == (si >> sh)
        pair_masks.append(same & ((ti & (2 * m - 1)) >= m) & ((si & (2 * m - 1)) < m))
        second_half.append((ri & (2 * m - 1)) >= m)
    diag_mask = ti == si
    amat = amat_ref[...]
    amat3 = jnp.concatenate([amat] * 3, axis=1)
    chunk_ids = [(b, c) for b in range(bb) for c in range(n_chunks)]
    heads = [slice(hh * HEAD_DIM, (hh + 1) * HEAD_DIM) for hh in range(N_HEADS)]

    e_all = {}
    for (b, c) in chunk_ids:
        rs = slice(b * tt + c * chunk, b * tt + (c + 1) * chunk)
        if chunk % BF16_ROWS == 0:
            args = _dot(amat3, jnp.concatenate([pt[rs] for pt in parts], axis=0))
        else:
            args = _dot(amat, parts[0][rs]) + _dot(amat, parts[1][rs]) + _dot(amat, parts[2][rs])
        e_all[b, c] = jnp.exp(args)

    scores = {}
    for (b, c) in chunk_ids:
        rs = slice(b * tt + c * chunk, b * tt + (c + 1) * chunk)
        for hh, hs in enumerate(heads):
            qh = q_all[rs, hs]
            kh = k_all[rs, hs]
            eh = e_all[b, c][:, hs]
            prods = []
            for lv in range(nlev):
                em = eh[lv * chunk:(lv + 1) * chunk].astype(_BF16)
                xm = jnp.where(second_half[lv], qh, kh) * em
                prods.append(_dot_nt(xm, xm))
            pm = jnp.where(diag_mask, _dot_nt(qh, kh), 0.0)
            for lv in range(nlev):
                pm = jnp.where(pair_masks[lv], prods[lv], pm)
            scores[b, c, hh] = pm.astype(_BF16)
        conv_rows(b, c * chunk, chunk)

    for b in range(bb):
        st = [st_ref[b, hh] for hh in range(N_HEADS)]
        for c in range(n_chunks):
            rs = slice(b * tt + c * chunk, b * tt + (c + 1) * chunk)
            for hh, hs in enumerate(heads):
                eh = e_all[b, c][:, hs]
                e_q = eh[nlev * chunk:(nlev + 1) * chunk]
                e_k = eh[(nlev + 1) * chunk:(nlev + 2) * chunk]
                qt = q_all[rs, hs] * e_q.astype(_BF16)
                kt = k_all[rs, hs] * e_k.astype(_BF16)
                vh = v_all[rs, hs]
                o = _dot(scores[b, c, hh], vh) + _dot_nt(qt, st[hh].astype(_BF16))
                st[hh] = st[hh] * e_q[chunk - 1:chunk, :] + _dot_tn(vh, kt)
                oa_ref[rs, hs] = o
        for hh in range(N_HEADS):
            st_ref[b, hh] = st[hh]

    for b in range(bb):
        for cc in range(N_LANE_SLABS):
            new_tail = xbuf_ref[b, cc, pl.ds(TAIL_ROW0 + tt, TAIL), :]
            xbuf_ref[b, cc, pl.ds(TAIL_ROW0, TAIL), :] = new_tail
    y = y_ref[...]
    mu = jnp.mean(y, axis=-1, keepdims=True)
    yc = y - mu
    y = yc * lax.rsqrt(jnp.mean(yc * yc, axis=-1, keepdims=True) + LN_EPS) * lng_ref[...] + lnb_ref[...]
    y = y * _sigmoid(y)
    y = _dot(y.astype(_BF16), wpw2_ref[...]) + bpw2_ref[...]
    gb = zpart(6)
    y = (y * (gb * _sigmoid(gb))).astype(_BF16)

    ga = zpart(3)
    ghead = ghead_ref[...]
    o_parts = []
    for hh, hs in enumerate(heads):
        oh = oa_ref[:, hs]
        oh = oh * lax.rsqrt(jnp.mean(oh * oh, axis=-1, keepdims=True) + RMS_EPS) * ghead
        gah = ga[:, hs]
        o_parts.append((oh * (gah * _sigmoid(gah))).astype(_BF16))
    o_a = jnp.concatenate(o_parts, axis=-1)

    out = _dot(o_a, wout_ref[0:D_A, :]) + _dot(y, wout_ref[D_A:D_A + D_B, :])
    out = out * lax.rsqrt(jnp.mean(out * out, axis=-1, keepdims=True) + RMS_EPS) * gpost_ref[...]
    h_new = x + out
    if n_layers > 1:
        hall_ref[gi, j] = h_new
    ho_ref[...] = h_new.reshape(bb, tt, D_MODEL)

    if single_tile:
        store_state()
    else:
        pl.when(j == n_tiles - 1)(store_state)


def _mixer_layers(layer0, n_layers, h, s0, c0, params, *, bb, tt, chunk, state_layer0=0):
    (lb_logits, norm_pre, norm_post, w_in, head_norm, conv_w, conv_b, ln_g, ln_b,
     w_pw2, b_pw2, w_out) = params
    batch, seq, _ = h.shape
    assert batch % bb == 0 and seq % tt == 0 and tt % chunk == 0
    shared_state = s0.shape[1] == 1 and batch > 1
    assert not shared_state or bb == 1
    amat = jnp.asarray(_decay_arg_matrix(chunk), _BF16)
    rows = bb * tt
    n_groups, n_tiles = batch // bb, seq // tt

    def state_idx(nd):
        if shared_state:
            return lambda l, i, j: (state_layer0 + l, 0) + (0,) * nd
        return lambda l, i, j: (state_layer0 + l, i) + (0,) * nd

    def whole(arr):
        nd = arr.ndim
        return pl.BlockSpec(arr.shape, lambda l, i, j: (0,) * nd)

    def per_layer(arr):
        tail = arr.shape[1:]
        nz = len(tail)
        return pl.BlockSpec((None,) + tail, lambda l, i, j: (layer0 + l,) + (0,) * nz)

    state_block = (None, bb, N_HEADS, HEAD_DIM, HEAD_DIM)
    tail_block = (None, bb, TAIL, D_B)
    in_specs = [
        pl.BlockSpec((bb, tt, D_MODEL), lambda l, i, j: (i, j, 0)),
        pl.BlockSpec(state_block, state_idx(3)),
        pl.BlockSpec(tail_block, state_idx(2)),
        whole(amat),
        whole(lb_logits),
        per_layer(norm_pre), per_layer(norm_post), per_layer(w_in), per_layer(head_norm),
        per_layer(conv_w), per_layer(conv_b), per_layer(ln_g), per_layer(ln_b),
        per_layer(w_pw2), per_layer(b_pw2), per_layer(w_out),
    ]
    last = n_layers - 1

    def h_out_idx(l, i, j):
        if n_layers == 1:
            return (i, j, 0)
        return (jnp.where(l == last, i, 0), jnp.where(l == last, j, 0), 0)

    out_specs = [
        pl.BlockSpec((bb, tt, D_MODEL), h_out_idx),
        pl.BlockSpec(state_block, lambda l, i, j: (l, i, 0, 0, 0)),
        pl.BlockSpec(tail_block, lambda l, i, j: (l, i, 0, 0)),
    ]
    out_shape = [
        jax.ShapeDtypeStruct((batch, seq, D_MODEL), h.dtype),
        jax.ShapeDtypeStruct((n_layers, batch, N_HEADS, HEAD_DIM, HEAD_DIM), h.dtype),
        jax.ShapeDtypeStruct((n_layers, batch, TAIL, D_B), h.dtype),
    ]
    hall_shape = (n_groups, n_tiles, rows, D_MODEL) if n_layers > 1 else (1, 1, SUBLANES, 128)
    scratch = [
        pltpu.VMEM((bb, N_HEADS, HEAD_DIM, HEAD_DIM), _F32),
        pltpu.VMEM((bb, N_LANE_SLABS, XROW0 + tt, LANES), _F32),
        pltpu.VMEM((rows, D_A), _F32),
        pltpu.VMEM((rows, D_B), _F32),
        pltpu.VMEM(hall_shape, _F32),
    ]
    return pl.pallas_call(
        functools.partial(_layer_kernel, layer0, n_layers, bb, tt, chunk, n_tiles),
        grid=(n_layers, n_groups, n_tiles),
        in_specs=in_specs,
        out_specs=out_specs,
        out_shape=out_shape,
        scratch_shapes=scratch,
        compiler_params=pltpu.CompilerParams(
            dimension_semantics=("arbitrary", "arbitrary", "arbitrary"),
            vmem_limit_bytes=VMEM_LIMIT_BYTES),
    )(h, s0, c0, amat, lb_logits, norm_pre, norm_post, w_in, head_norm, conv_w, conv_b,
      ln_g, ln_b, w_pw2, b_pw2, w_out)


def kernel(x_prompt, x_sample, state_hgrn, state_conv, meta_tokens, norm_pre, norm_post, w_in,
           lb_logits, head_norm, conv_w, conv_b, conv_ln_g, conv_ln_b, w_pw2, b_pw2, w_out):
    depth = w_in.shape[0]
    row3 = lambda a: a.reshape(depth, 1, a.shape[-1])
    params = (lb_logits, row3(norm_pre), row3(norm_post), w_in.astype(_BF16), row3(head_norm),
              conv_w, row3(conv_b), row3(conv_ln_g), row3(conv_ln_b), w_pw2.astype(_BF16),
              row3(b_pw2), w_out.astype(_BF16))
    dtype = x_prompt.dtype

    n_dec = x_sample.shape[1]
    y_sample, s_sample, c_sample = _mixer_layers(
        0, depth, x_sample, state_hgrn, state_conv, params, bb=16, tt=n_dec, chunk=n_dec)

    hm = meta_tokens[None].astype(dtype)
    zero_s = jnp.zeros((depth, 1, N_HEADS, HEAD_DIM, HEAD_DIM), dtype)
    zero_c = jnp.zeros((depth, 1, TAIL, D_B), dtype)
    _, s_meta, c_meta = _mixer_layers(0, depth, hm, zero_s, zero_c, params, bb=1, tt=N_META,
                                      chunk=N_META)
    hp = x_prompt
    sp_l, cp_l = [], []
    for l in range(depth):
        hp, sp, cp = _mixer_layers(l, 1, hp, s_meta, c_meta, params, bb=1, tt=PROMPT_TILE,
                                   chunk=PROMPT_CHUNK, state_layer0=l)
        sp_l.append(sp)
        cp_l.append(cp)
    return (hp, y_sample, jnp.concatenate(sp_l), jnp.concatenate(cp_l), s_sample, c_sample)
```

```python
import functools

import numpy as np
import jax
import jax.numpy as jnp
from jax import lax
from jax.experimental import pallas as pl
from jax.experimental.pallas import tpu as pltpu

D_MODEL = 1024
D_A = 512
HEAD_DIM = 128
N_HEADS = D_A // HEAD_DIM
D_B = 512
CONV_WIDTH = 31
TAIL = CONV_WIDTH - 1
N_META = 16
D_IN = 4 * D_A + 3 * D_B
RMS_EPS = 1e-6
LN_EPS = 1e-5
LB_FLOOR = 1e-30

SUBLANES = 8
LANES = 128
BF16_ROWS = 16
N_LANE_SLABS = D_B // LANES
XROW0 = 32
TAIL_ROW0 = XROW0 - TAIL

VMEM_LIMIT_BYTES = 56 * 1024 * 1024
PROMPT_TILE = 512
PROMPT_CHUNK = 64

_F32 = jnp.float32
_BF16 = jnp.bfloat16


def _levels(chunk):
    out = []
    m = chunk // 2
    while m >= 1:
        out.append(m)
        m //= 2
    return tuple(out)


def _decay_arg_matrix(chunk):
    blocks = []
    for m in _levels(chunk):
        a = np.zeros((chunk, chunk), np.float32)
        for r in range(chunk):
            mid = (r // (2 * m)) * 2 * m + m - 1
            if r > mid:
                a[r, mid + 1:r + 1] = 1.0
            else:
                a[r, r + 1:mid + 1] = 1.0
        blocks.append(a)
    blocks.append(np.tril(np.ones((chunk, chunk), np.float32)))
    blocks.append(np.triu(np.ones((chunk, chunk), np.float32), 1))
    return np.concatenate(blocks, axis=0)


def _sigmoid(x):
    return 1.0 / (1.0 + jnp.exp(-x))


def _dot(a, b):
    return jnp.dot(a, b, preferred_element_type=_F32)


def _dot_nt(a, b):
    return lax.dot_general(a, b, (((1,), (1,)), ((), ())), preferred_element_type=_F32)


def _dot_tn(a, b):
    return lax.dot_general(a, b, (((0,), (0,)), ((), ())), preferred_element_type=_F32)


def _split3(x):
    hi = x.astype(_BF16)
    r1 = x - hi.astype(_F32)
    mid = r1.astype(_BF16)
    lo = (r1 - mid.astype(_F32)).astype(_BF16)
    return hi, mid, lo


def _layer_kernel(layer0, n_layers, bb, tt, chunk, n_tiles,
                  h_ref, s0_ref, c0_ref, amat_ref, lbl_ref, gpre_ref, gpost_ref, win_ref,
                  ghead_ref, convw_ref, convb_ref, lng_ref, lnb_ref, wpw2_ref, bpw2_ref,
                  wout_ref, ho_ref, so_ref, co_ref, st_ref, xbuf_ref, oa_ref, y_ref, hall_ref):
    li = pl.program_id(0)
    gi = pl.program_id(1)
    j = pl.program_id(2)
    layer = layer0 if n_layers == 1 else layer0 + li
    rows = bb * tt
    n_chunks = tt // chunk
    levels = _levels(chunk)
    nlev = len(levels)

    single_tile = n_tiles == 1
    time_major_conv = single_tile and tt == SUBLANES and bb % SUBLANES == 0

    def load_state():
        for b in range(bb):
            for hh in range(N_HEADS):
                st_ref[b, hh] = s0_ref[b, hh].T
            if not time_major_conv:
                for cc in range(N_LANE_SLABS):
                    xbuf_ref[b, cc, pl.ds(TAIL_ROW0, TAIL), :] = c0_ref[b][:, cc * LANES:(cc + 1) * LANES]
        if time_major_conv:
            xbuf_ref[0:TAIL] = c0_ref[...]

    def store_state():
        for b in range(bb):
            for hh in range(N_HEADS):
                so_ref[b, hh] = st_ref[b, hh].T
            if not time_major_conv:
                for cc in range(N_LANE_SLABS):
                    co_ref[b, :, cc * LANES:(cc + 1) * LANES] = xbuf_ref[b, cc, pl.ds(TAIL_ROW0, TAIL), :]
        if time_major_conv:
            co_ref[...] = xbuf_ref[tt:tt + TAIL]

    if not single_tile:
        pl.when(j == 0)(load_state)

    lg = lbl_ref[...]
    pe = jnp.exp(lg - jnp.max(lg, axis=0, keepdims=True))
    p = pe / jnp.sum(pe, axis=0, keepdims=True)
    upto = lax.broadcasted_iota(jnp.int32, lg.shape, 0) <= layer
    lb = jnp.sum(jnp.where(upto, p, 0.0), axis=0, keepdims=True) - p[0:1, :]
    lb_floor = jnp.maximum(lb, LB_FLOOR)
    one_minus_lb = 1.0 - lb

    if n_layers == 1:
        x = h_ref[...].reshape(rows, D_MODEL)
    else:
        @pl.when(li == 0)
        def _first_layer_input():
            hall_ref[gi, j] = h_ref[...].reshape(rows, D_MODEL)
        x = hall_ref[gi, j]
    if single_tile:
        load_state()
    u = x * lax.rsqrt(jnp.mean(x * x, axis=-1, keepdims=True) + RMS_EPS) * gpre_ref[...]
    u = u.astype(_BF16)

    def zpart(i):
        return _dot(u, win_ref[:, i * 512:(i + 1) * 512])

    a = zpart(4) * _sigmoid(zpart(5))
    convw = convw_ref[...]
    convb = convb_ref[...]
    if time_major_conv:
        for cc in range(N_LANE_SLABS):
            y_ref[cc] = a[:, cc * LANES:(cc + 1) * LANES]
        for t in range(tt):
            for cc in range(N_LANE_SLABS):
                xbuf_ref[TAIL + t, :, cc * LANES:(cc + 1) * LANES] = y_ref[cc, pl.ds(t, bb, stride=tt), :]
        for t in range(tt):
            acc = None
            for tap in range(CONV_WIDTH):
                term = xbuf_ref[t + tap] * convw[tap:tap + 1, :]
                acc = term if acc is None else acc + term
            acc = acc + convb
            for cc in range(N_LANE_SLABS):
                y_ref[cc, pl.ds(t, bb, stride=tt), :] = acc[:, cc * LANES:(cc + 1) * LANES]
    else:
        for b in range(bb):
            for cc in range(N_LANE_SLABS):
                xbuf_ref[b, cc, pl.ds(XROW0, tt), :] = a[b * tt:(b + 1) * tt, cc * LANES:(cc + 1) * LANES]

    def conv_rows(b, t0, nt):
        for cc in range(N_LANE_SLABS):
            ls = slice(cc * LANES, (cc + 1) * LANES)
            acc = None
            for tap in range(CONV_WIDTH):
                term = xbuf_ref[b, cc, pl.ds(TAIL_ROW0 + t0 + tap, nt), :] * convw[tap:tap + 1, ls]
                acc = term if acc is None else acc + term
            y_ref[b * tt + t0:b * tt + t0 + nt, ls] = acc + convb[:, ls]

    q_all = zpart(0).astype(_BF16)
    zf = zpart(1)
    v_all = zpart(2).astype(_BF16)
    t = jnp.exp(-jnp.abs(zf))
    r = 1.0 / (1.0 + t)
    tr = t * r
    pos = zf >= 0
    sig = jnp.where(pos, r, tr)
    sneg = jnp.where(pos, tr, r)
    k_all = (one_minus_lb * sneg).astype(_BF16)
    logf = jnp.log(sig + lb_floor * sneg)
    parts = _split3(logf)

    ti = lax.broadcasted_iota(jnp.int32, (chunk, chunk), 0)
    si = lax.broadcasted_iota(jnp.int32, (chunk, chunk), 1)
    ri = lax.broadcasted_iota(jnp.int32, (chunk, HEAD_DIM), 0)
    pair_masks = []
    second_half = []
    for m in levels:
        sh = int(np.log2(2 * m))
        same = (ti >> sh) == (si >> sh)
        pair_masks.append(same & ((ti & (2 * m - 1)) >= m) & ((si & (2 * m - 1)) < m))
        second_half.append((ri & (2 * m - 1)) >= m)
    diag_mask = ti == si
    amat = amat_ref[...]
    amat3 = jnp.concatenate([amat] * 3, axis=1)
    chunk_ids = [(b, c) for b in range(bb) for c in range(n_chunks)]
    heads = [slice(hh * HEAD_DIM, (hh + 1) * HEAD_DIM) for hh in range(N_HEADS)]

    e_all = {}
    for (b, c) in chunk_ids:
        rs = slice(b * tt + c * chunk, b * tt + (c + 1) * chunk)
        if chunk % BF16_ROWS == 0:
            args = _dot(amat3, jnp.concatenate([pt[rs] for pt in parts], axis=0))
        else:
            args = _dot(amat, parts[0][rs]) + _dot(amat, parts[1][rs]) + _dot(amat, parts[2][rs])
        e_all[b, c] = jnp.exp(args)

    scores = {}
    for (b, c) in chunk_ids:
        rs = slice(b * tt + c * chunk, b * tt + (c + 1) * chunk)
        for hh, hs in enumerate(heads):
            qh = q_all[rs, hs]
            kh = k_all[rs, hs]
            eh = e_all[b, c][:, hs]
            prods = []
            for lv in range(nlev):
                em = eh[lv * chunk:(lv + 1) * chunk].astype(_BF16)
                xm = jnp.where(second_half[lv], qh, kh) * em
                prods.append(_dot_nt(xm, xm))
            pm = jnp.where(diag_mask, _dot_nt(qh, kh), 0.0)
            for lv in range(nlev):
                pm = jnp.where(pair_masks[lv], prods[lv], pm)
            scores[b, c, hh] = pm.astype(_BF16)
        if not time_major_conv:
            conv_rows(b, c * chunk, chunk)

    for b in range(bb):
        st = [st_ref[b, hh] for hh in range(N_HEADS)]
        for c in range(n_chunks):
            rs = slice(b * tt + c * chunk, b * tt + (c + 1) * chunk)
            for hh, hs in enumerate(heads):
                eh = e_all[b, c][:, hs]
                e_q = eh[nlev * chunk:(nlev + 1) * chunk]
                e_k = eh[(nlev + 1) * chunk:(nlev + 2) * chunk]
                qt = q_all[rs, hs] * e_q.astype(_BF16)
                kt = k_all[rs, hs] * e_k.astype(_BF16)
                vh = v_all[rs, hs]
                o = _dot(scores[b, c, hh], vh) + _dot_nt(qt, st[hh].astype(_BF16))
                st[hh] = st[hh] * e_q[chunk - 1:chunk, :] + _dot_tn(vh, kt)
                oa_ref[rs, hs] = o
        for hh in range(N_HEADS):
            st_ref[b, hh] = st[hh]

    if time_major_conv:
        y = jnp.concatenate([y_ref[cc] for cc in range(N_LANE_SLABS)], axis=1)
    else:
        for b in range(bb):
            for cc in range(N_LANE_SLABS):
                new_tail = xbuf_ref[b, cc, pl.ds(TAIL_ROW0 + tt, TAIL), :]
                xbuf_ref[b, cc, pl.ds(TAIL_ROW0, TAIL), :] = new_tail
        y = y_ref[...]
    mu = jnp.mean(y, axis=-1, keepdims=True)
    yc = y - mu
    y = yc * lax.rsqrt(jnp.mean(yc * yc, axis=-1, keepdims=True) + LN_EPS) * lng_ref[...] + lnb_ref[...]
    y = y * _sigmoid(y)
    y = _dot(y.astype(_BF16), wpw2_ref[...]) + bpw2_ref[...]
    gb = zpart(6)
    y = (y * (gb * _sigmoid(gb))).astype(_BF16)

    ga = zpart(3)
    ghead = ghead_ref[...]
    o_parts = []
    for hh, hs in enumerate(heads):
        oh = oa_ref[:, hs]
        oh = oh * lax.rsqrt(jnp.mean(oh * oh, axis=-1, keepdims=True) + RMS_EPS) * ghead
        gah = ga[:, hs]
        o_parts.append((oh * (gah * _sigmoid(gah))).astype(_BF16))
    o_a = jnp.concatenate(o_parts, axis=-1)

    out = _dot(o_a, wout_ref[0:D_A, :]) + _dot(y, wout_ref[D_A:D_A + D_B, :])
    out = out * lax.rsqrt(jnp.mean(out * out, axis=-1, keepdims=True) + RMS_EPS) * gpost_ref[...]
    h_new = x + out
    if n_layers > 1:
        hall_ref[gi, j] = h_new
    ho_ref[...] = h_new.reshape(bb, tt, D_MODEL)

    if single_tile:
        store_state()
    else:
        pl.when(j == n_tiles - 1)(store_state)


def _mixer_layers(layer0, n_layers, h, s0, c0, params, *, bb, tt, chunk, state_layer0=0):
    (lb_logits, norm_pre, norm_post, w_in, head_norm, conv_w, conv_b, ln_g, ln_b,
     w_pw2, b_pw2, w_out) = params
    batch, seq, _ = h.shape
    assert batch % bb == 0 and seq % tt == 0 and tt % chunk == 0
    shared_state = s0.shape[1] == 1 and batch > 1
    assert not shared_state or bb == 1
    amat = jnp.asarray(_decay_arg_matrix(chunk), _BF16)
    rows = bb * tt
    n_groups, n_tiles = batch // bb, seq // tt
    time_major_conv = n_tiles == 1 and tt == SUBLANES and bb % SUBLANES == 0

    def state_idx(nd):
        if shared_state:
            return lambda l, i, j: (state_layer0 + l, 0) + (0,) * nd
        return lambda l, i, j: (state_layer0 + l, i) + (0,) * nd

    if time_major_conv:
        c0 = jnp.swapaxes(c0, 1, 2)
        tail_block = (None, TAIL, bb, D_B)
        tail_in_idx = lambda l, i, j: (state_layer0 + l, 0, i, 0)
        tail_out_idx = lambda l, i, j: (l, 0, i, 0)
        tail_out_shape = (n_layers, TAIL, batch, D_B)
    else:
        tail_block = (None, bb, TAIL, D_B)
        tail_in_idx = state_idx(2)
        tail_out_idx = lambda l, i, j: (l, i, 0, 0)
        tail_out_shape = (n_layers, batch, TAIL, D_B)

    def whole(arr):
        nd = arr.ndim
        return pl.BlockSpec(arr.shape, lambda l, i, j: (0,) * nd)

    def per_layer(arr):
        tail = arr.shape[1:]
        nz = len(tail)
        return pl.BlockSpec((None,) + tail, lambda l, i, j: (layer0 + l,) + (0,) * nz)

    state_block = (None, bb, N_HEADS, HEAD_DIM, HEAD_DIM)
    in_specs = [
        pl.BlockSpec((bb, tt, D_MODEL), lambda l, i, j: (i, j, 0)),
        pl.BlockSpec(state_block, state_idx(3)),
        pl.BlockSpec(tail_block, tail_in_idx),
        whole(amat),
        whole(lb_logits),
        per_layer(norm_pre), per_layer(norm_post), per_layer(w_in), per_layer(head_norm),
        per_layer(conv_w), per_layer(conv_b), per_layer(ln_g), per_layer(ln_b),
        per_layer(w_pw2), per_layer(b_pw2), per_layer(w_out),
    ]
    last = n_layers - 1

    def h_out_idx(l, i, j):
        if n_layers == 1:
            return (i, j, 0)
        return (jnp.where(l == last, i, 0), jnp.where(l == last, j, 0), 0)

    out_specs = [
        pl.BlockSpec((bb, tt, D_MODEL), h_out_idx),
        pl.BlockSpec(state_block, lambda l, i, j: (l, i, 0, 0, 0)),
        pl.BlockSpec(tail_block, tail_out_idx),
    ]
    out_shape = [
        jax.ShapeDtypeStruct((batch, seq, D_MODEL), h.dtype),
        jax.ShapeDtypeStruct((n_layers, batch, N_HEADS, HEAD_DIM, HEAD_DIM), h.dtype),
        jax.ShapeDtypeStruct(tail_out_shape, h.dtype),
    ]
    hall_shape = (n_groups, n_tiles, rows, D_MODEL) if n_layers > 1 else (1, 1, SUBLANES, 128)
    scratch = [
        pltpu.VMEM((bb, N_HEADS, HEAD_DIM, HEAD_DIM), _F32),
        pltpu.VMEM((TAIL + tt, bb, D_B) if time_major_conv
                   else (bb, N_LANE_SLABS, XROW0 + tt, LANES), _F32),
        pltpu.VMEM((rows, D_A), _F32),
        pltpu.VMEM((N_LANE_SLABS, rows, LANES) if time_major_conv else (rows, D_B), _F32),
        pltpu.VMEM(hall_shape, _F32),
    ]
    h_out, s_out, c_out = pl.pallas_call(
        functools.partial(_layer_kernel, layer0, n_layers, bb, tt, chunk, n_tiles),
        grid=(n_layers, n_groups, n_tiles),
        in_specs=in_specs,
        out_specs=out_specs,
        out_shape=out_shape,
        scratch_shapes=scratch,
        compiler_params=pltpu.CompilerParams(
            dimension_semantics=("arbitrary", "arbitrary", "arbitrary"),
            vmem_limit_bytes=VMEM_LIMIT_BYTES),
    )(h, s0, c0, amat, lb_logits, norm_pre, norm_post, w_in, head_norm, conv_w, conv_b,
      ln_g, ln_b, w_pw2, b_pw2, w_out)
    if time_major_conv:
        c_out = jnp.swapaxes(c_out, 1, 2)
    return h_out, s_out, c_out


def kernel(x_prompt, x_sample, state_hgrn, state_conv, meta_tokens, norm_pre, norm_post, w_in,
           lb_logits, head_norm, conv_w, conv_b, conv_ln_g, conv_ln_b, w_pw2, b_pw2, w_out):
    depth = w_in.shape[0]
    row3 = lambda a: a.reshape(depth, 1, a.shape[-1])
    params = (lb_logits, row3(norm_pre), row3(norm_post), w_in.astype(_BF16), row3(head_norm),
              conv_w, row3(conv_b), row3(conv_ln_g), row3(conv_ln_b), w_pw2.astype(_BF16),
              row3(b_pw2), w_out.astype(_BF16))
    dtype = x_prompt.dtype

    n_dec = x_sample.shape[1]
    y_sample, s_sample, c_sample = _mixer_layers(
        0, depth, x_sample, state_hgrn, state_conv, params, bb=16, tt=n_dec, chunk=n_dec)

    hm = meta_tokens[None].astype(dtype)
    zero_s = jnp.zeros((depth, 1, N_HEADS, HEAD_DIM, HEAD_DIM), dtype)
    zero_c = jnp.zeros((depth, 1, TAIL, D_B), dtype)
    _, s_meta, c_meta = _mixer_layers(0, depth, hm, zero_s, zero_c, params, bb=1, tt=N_META,
                                      chunk=N_META)
    hp = x_prompt
    sp_l, cp_l = [], []
    for l in range(depth):
        hp, sp, cp = _mixer_layers(l, 1, hp, s_meta, c_meta, params, bb=1, tt=PROMPT_TILE,
                                   chunk=PROMPT_CHUNK, state_layer0=l)
        sp_l.append(sp)
        cp_l.append(cp)
    return (hp, y_sample, jnp.concatenate(sp_l), jnp.concatenate(cp_l), s_sample, c_sample)
```

```python
import functools

import numpy as np
import jax
import jax.numpy as jnp
from jax import lax
from jax.experimental import pallas as pl
from jax.experimental.pallas import tpu as pltpu

D_MODEL = 1024
D_A = 512
HEAD_DIM = 128
N_HEADS = D_A // HEAD_DIM
D_B = 512
CONV_WIDTH = 31
TAIL = CONV_WIDTH - 1
N_META = 16
D_IN = 4 * D_A + 3 * D_B
RMS_EPS = 1e-6
LN_EPS = 1e-5
LB_FLOOR = 1e-30
NEG_LOG2_E = -1.4426950408889634

SUBLANES = 8
LANES = 128
BF16_ROWS = 16
N_LANE_SLABS = D_B // LANES
XROW0 = 32
TAIL_ROW0 = XROW0 - TAIL

VMEM_LIMIT_BYTES = 56 * 1024 * 1024
PROMPT_TILE = 512
PROMPT_CHUNK = 64

_F32 = jnp.float32
_BF16 = jnp.bfloat16


def _levels(chunk):
    out = []
    m = chunk // 2
    while m >= 1:
        out.append(m)
        m //= 2
    return tuple(out)


def _decay_arg_matrix(chunk):
    blocks = []
    for m in _levels(chunk):
        a = np.zeros((chunk, chunk), np.float32)
        for r in range(chunk):
            mid = (r // (2 * m)) * 2 * m + m - 1
            if r > mid:
                a[r, mid + 1:r + 1] = 1.0
            else:
                a[r, r + 1:mid + 1] = 1.0
        blocks.append(a)
    blocks.append(np.tril(np.ones((chunk, chunk), np.float32)))
    blocks.append(np.triu(np.ones((chunk, chunk), np.float32), 1))
    return np.concatenate(blocks, axis=0)


def _sigmoid(x):
    return 0.5 * jnp.tanh(0.5 * x) + 0.5


def _silu(x):
    h = 0.5 * x
    return h * jnp.tanh(h) + h


def _dot(a, b):
    return jnp.dot(a, b, preferred_element_type=_F32)


def _dot_nt(a, b):
    return lax.dot_general(a, b, (((1,), (1,)), ((), ())), preferred_element_type=_F32)


def _dot_tn(a, b):
    return lax.dot_general(a, b, (((0,), (0,)), ((), ())), preferred_element_type=_F32)


def _split3(x):
    hi = x.astype(_BF16)
    r1 = x - hi.astype(_F32)
    mid = r1.astype(_BF16)
    lo = (r1 - mid.astype(_F32)).astype(_BF16)
    return hi, mid, lo


def _layer_kernel(layer0, n_layers, bb, tt, chunk, n_tiles,
                  h_ref, s0_ref, c0_ref, amat_ref, lbl_ref, gpre_ref, gpost_ref, win_ref,
                  ghead_ref, convw_ref, convb_ref, lng_ref, lnb_ref, wpw2_ref, bpw2_ref,
                  wout_ref, ho_ref, so_ref, co_ref, st_ref, xbuf_ref, oa_ref, y_ref, hall_ref):
    li = pl.program_id(0)
    gi = pl.program_id(1)
    j = pl.program_id(2)
    layer = layer0 if n_layers == 1 else layer0 + li
    rows = bb * tt
    n_chunks = tt // chunk
    levels = _levels(chunk)
    nlev = len(levels)

    single_tile = n_tiles == 1
    time_major_conv = single_tile and tt == SUBLANES and bb % SUBLANES == 0

    def load_state():
        for b in range(bb):
            for hh in range(N_HEADS):
                st_ref[b, hh] = s0_ref[b, hh].T
            if not time_major_conv:
                for cc in range(N_LANE_SLABS):
                    xbuf_ref[b, cc, pl.ds(TAIL_ROW0, TAIL), :] = c0_ref[b][:, cc * LANES:(cc + 1) * LANES]
        if time_major_conv:
            xbuf_ref[0:TAIL] = c0_ref[...]

    def store_state():
        for b in range(bb):
            for hh in range(N_HEADS):
                so_ref[b, hh] = st_ref[b, hh].T
            if not time_major_conv:
                for cc in range(N_LANE_SLABS):
                    co_ref[b, :, cc * LANES:(cc + 1) * LANES] = xbuf_ref[b, cc, pl.ds(TAIL_ROW0, TAIL), :]
        if time_major_conv:
            co_ref[...] = xbuf_ref[tt:tt + TAIL]

    if not single_tile:
        pl.when(j == 0)(load_state)

    lg = lbl_ref[...]
    pe = jnp.exp(lg - jnp.max(lg, axis=0, keepdims=True))
    p = pe / jnp.sum(pe, axis=0, keepdims=True)
    upto = lax.broadcasted_iota(jnp.int32, lg.shape, 0) <= layer
    lb = jnp.sum(jnp.where(upto, p, 0.0), axis=0, keepdims=True) - p[0:1, :]
    lb_floor = jnp.maximum(lb, LB_FLOOR)
    one_minus_lb = 1.0 - lb

    if n_layers == 1:
        x = h_ref[...].reshape(rows, D_MODEL)
    else:
        @pl.when(li == 0)
        def _first_layer_input():
            hall_ref[gi, j] = h_ref[...].reshape(rows, D_MODEL)
        x = hall_ref[gi, j]
    if single_tile:
        load_state()
    u = x * lax.rsqrt(jnp.mean(x * x, axis=-1, keepdims=True) + RMS_EPS) * gpre_ref[...]
    u = u.astype(_BF16)

    def zpart(i):
        return _dot(u, win_ref[:, i * 512:(i + 1) * 512])

    a = zpart(4) * _sigmoid(zpart(5))
    convw = convw_ref[...]
    convb = convb_ref[...]
    if time_major_conv:
        for cc in range(N_LANE_SLABS):
            y_ref[cc] = a[:, cc * LANES:(cc + 1) * LANES]
        for t in range(tt):
            for cc in range(N_LANE_SLABS):
                xbuf_ref[TAIL + t, :, cc * LANES:(cc + 1) * LANES] = y_ref[cc, pl.ds(t, bb, stride=tt), :]
        for t in range(tt):
            acc = None
            for tap in range(CONV_WIDTH):
                term = xbuf_ref[t + tap] * convw[tap:tap + 1, :]
                acc = term if acc is None else acc + term
            acc = acc + convb
            for cc in range(N_LANE_SLABS):
                y_ref[cc, pl.ds(t, bb, stride=tt), :] = acc[:, cc * LANES:(cc + 1) * LANES]
    else:
        for b in range(bb):
            for cc in range(N_LANE_SLABS):
                xbuf_ref[b, cc, pl.ds(XROW0, tt), :] = a[b * tt:(b + 1) * tt, cc * LANES:(cc + 1) * LANES]

    def conv_rows(b, t0, nt):
        for cc in range(N_LANE_SLABS):
            ls = slice(cc * LANES, (cc + 1) * LANES)
            acc = None
            for tap in range(CONV_WIDTH):
                term = xbuf_ref[b, cc, pl.ds(TAIL_ROW0 + t0 + tap, nt), :] * convw[tap:tap + 1, ls]
                acc = term if acc is None else acc + term
            y_ref[b * tt + t0:b * tt + t0 + nt, ls] = acc + convb[:, ls]

    q_all = zpart(0).astype(_BF16)
    zf = zpart(1)
    v_all = zpart(2).astype(_BF16)
    t = jnp.exp2(jnp.abs(zf) * NEG_LOG2_E)
    r = 1.0 / (1.0 + t)
    tr = t * r
    pos = zf >= 0
    sig = jnp.where(pos, r, tr)
    sneg = jnp.where(pos, tr, r)
    k_all = (one_minus_lb * sneg).astype(_BF16)
    logf = jnp.log2(sig + lb_floor * sneg)
    parts = _split3(logf)

    ti = lax.broadcasted_iota(jnp.int32, (chunk, chunk), 0)
    si = lax.broadcasted_iota(jnp.int32, (chunk, chunk), 1)
    ri = lax.broadcasted_iota(jnp.int32, (chunk, HEAD_DIM), 0)
    pair_masks = []
    second_half = []
    for m in levels:
        sh = int(np.log2(2 * m))
        same = (ti >> sh) == (si >> sh)
        pair_masks.append(same & ((ti & (2 * m - 1)) >= m) & ((si & (2 * m - 1)) < m))
        second_half.append((ri & (2 * m - 1)) >= m)
    diag_mask = ti == si
    amat = amat_ref[...]
    amat3 = jnp.concatenate([amat] * 3, axis=1)
    chunk_ids = [(b, c) for b in range(bb) for c in range(n_chunks)]
    heads = [slice(hh * HEAD_DIM, (hh + 1) * HEAD_DIM) for hh in range(N_HEADS)]

    e_all = {}
    for (b, c) in chunk_ids:
        rs = slice(b * tt + c * chunk, b * tt + (c + 1) * chunk)
        if chunk % BF16_ROWS == 0:
            args = _dot(amat3, jnp.concatenate([pt[rs] for pt in parts], axis=0))
        else:
            args = _dot(amat, parts[0][rs]) + _dot(amat, parts[1][rs]) + _dot(amat, parts[2][rs])
        e_all[b, c] = jnp.exp2(args)

    scores = {}
    for (b, c) in chunk_ids:
        rs = slice(b * tt + c * chunk, b * tt + (c + 1) * chunk)
        for hh, hs in enumerate(heads):
            qh = q_all[rs, hs]
            kh = k_all[rs, hs]
            eh = e_all[b, c][:, hs]
            prods = []
            for lv in range(nlev):
                em = eh[lv * chunk:(lv + 1) * chunk].astype(_BF16)
                xm = jnp.where(second_half[lv], qh, kh) * em
                prods.append(_dot_nt(xm, xm))
            pm = jnp.where(diag_mask, _dot_nt(qh, kh), 0.0)
            for lv in range(nlev):
                pm = jnp.where(pair_masks[lv], prods[lv], pm)
            scores[b, c, hh] = pm.astype(_BF16)
        if not time_major_conv:
            conv_rows(b, c * chunk, chunk)

    for b in range(bb):
        st = [st_ref[b, hh] for hh in range(N_HEADS)]
        for c in range(n_chunks):
            rs = slice(b * tt + c * chunk, b * tt + (c + 1) * chunk)
            for hh, hs in enumerate(heads):
                eh = e_all[b, c][:, hs]
                e_q = eh[nlev * chunk:(nlev + 1) * chunk]
                e_k = eh[(nlev + 1) * chunk:(nlev + 2) * chunk]
                qt = q_all[rs, hs] * e_q.astype(_BF16)
                kt = k_all[rs, hs] * e_k.astype(_BF16)
                vh = v_all[rs, hs]
                o = _dot(scores[b, c, hh], vh) + _dot_nt(qt, st[hh].astype(_BF16))
                st[hh] = st[hh] * e_q[chunk - 1:chunk, :] + _dot_tn(vh, kt)
                oa_ref[rs, hs] = o
        for hh in range(N_HEADS):
            st_ref[b, hh] = st[hh]

    if time_major_conv:
        y = jnp.concatenate([y_ref[cc] for cc in range(N_LANE_SLABS)], axis=1)
    else:
        for b in range(bb):
            for cc in range(N_LANE_SLABS):
                new_tail = xbuf_ref[b, cc, pl.ds(TAIL_ROW0 + tt, TAIL), :]
                xbuf_ref[b, cc, pl.ds(TAIL_ROW0, TAIL), :] = new_tail
        y = y_ref[...]
    mu = jnp.mean(y, axis=-1, keepdims=True)
    yc = y - mu
    y = yc * lax.rsqrt(jnp.mean(yc * yc, axis=-1, keepdims=True) + LN_EPS) * lng_ref[...] + lnb_ref[...]
    y = _silu(y)
    y = _dot(y.astype(_BF16), wpw2_ref[...]) + bpw2_ref[...]
    gb = zpart(6)
    y = (y * _silu(gb)).astype(_BF16)

    ga = zpart(3)
    ghead = ghead_ref[...]
    o_parts = []
    for hh, hs in enumerate(heads):
        oh = oa_ref[:, hs]
        oh = oh * lax.rsqrt(jnp.mean(oh * oh, axis=-1, keepdims=True) + RMS_EPS) * ghead
        gah = ga[:, hs]
        o_parts.append((oh * _silu(gah)).astype(_BF16))
    o_a = jnp.concatenate(o_parts, axis=-1)

    out = _dot(o_a, wout_ref[0:D_A, :]) + _dot(y, wout_ref[D_A:D_A + D_B, :])
    out = out * lax.rsqrt(jnp.mean(out * out, axis=-1, keepdims=True) + RMS_EPS) * gpost_ref[...]
    h_new = x + out
    if n_layers > 1:
        hall_ref[gi, j] = h_new
    ho_ref[...] = h_new.reshape(bb, tt, D_MODEL)

    if single_tile:
        store_state()
    else:
        pl.when(j == n_tiles - 1)(store_state)


def _mixer_layers(layer0, n_layers, h, s0, c0, params, *, bb, tt, chunk, state_layer0=0):
    (lb_logits, norm_pre, norm_post, w_in, head_norm, conv_w, conv_b, ln_g, ln_b,
     w_pw2, b_pw2, w_out) = params
    batch, seq, _ = h.shape
    assert batch % bb == 0 and seq % tt == 0 and tt % chunk == 0
    shared_state = s0.shape[1] == 1 and batch > 1
    assert not shared_state or bb == 1
    amat = jnp.asarray(_decay_arg_matrix(chunk), _BF16)
    rows = bb * tt
    n_groups, n_tiles = batch // bb, seq // tt
    time_major_conv = n_tiles == 1 and tt == SUBLANES and bb % SUBLANES == 0

    def state_idx(nd):
        if shared_state:
            return lambda l, i, j: (state_layer0 + l, 0) + (0,) * nd
        return lambda l, i, j: (state_layer0 + l, i) + (0,) * nd

    if time_major_conv:
        c0 = jnp.swapaxes(c0, 1, 2)
        tail_block = (None, TAIL, bb, D_B)
        tail_in_idx = lambda l, i, j: (state_layer0 + l, 0, i, 0)
        tail_out_idx = lambda l, i, j: (l, 0, i, 0)
        tail_out_shape = (n_layers, TAIL, batch, D_B)
    else:
        tail_block = (None, bb, TAIL, D_B)
        tail_in_idx = state_idx(2)
        tail_out_idx = lambda l, i, j: (l, i, 0, 0)
        tail_out_shape = (n_layers, batch, TAIL, D_B)

    def whole(arr):
        nd = arr.ndim
        return pl.BlockSpec(arr.shape, lambda l, i, j: (0,) * nd)

    def per_layer(arr):
        tail = arr.shape[1:]
        nz = len(tail)
        return pl.BlockSpec((None,) + tail, lambda l, i, j: (layer0 + l,) + (0,) * nz)

    state_block = (None, bb, N_HEADS, HEAD_DIM, HEAD_DIM)
    in_specs = [
        pl.BlockSpec((bb, tt, D_MODEL), lambda l, i, j: (i, j, 0)),
        pl.BlockSpec(state_block, state_idx(3)),
        pl.BlockSpec(tail_block, tail_in_idx),
        whole(amat),
        whole(lb_logits),
        per_layer(norm_pre), per_layer(norm_post), per_layer(w_in), per_layer(head_norm),
        per_layer(conv_w), per_layer(conv_b), per_layer(ln_g), per_layer(ln_b),
        per_layer(w_pw2), per_layer(b_pw2), per_layer(w_out),
    ]
    last = n_layers - 1

    def h_out_idx(l, i, j):
        if n_layers == 1:
            return (i, j, 0)
        return (jnp.where(l == last, i, 0), jnp.where(l == last, j, 0), 0)

    out_specs = [
        pl.BlockSpec((bb, tt, D_MODEL), h_out_idx),
        pl.BlockSpec(state_block, lambda l, i, j: (l, i, 0, 0, 0)),
        pl.BlockSpec(tail_block, tail_out_idx),
    ]
    out_shape = [
        jax.ShapeDtypeStruct((batch, seq, D_MODEL), h.dtype),
        jax.ShapeDtypeStruct((n_layers, batch, N_HEADS, HEAD_DIM, HEAD_DIM), h.dtype),
        jax.ShapeDtypeStruct(tail_out_shape, h.dtype),
    ]
    hall_shape = (n_groups, n_tiles, rows, D_MODEL) if n_layers > 1 else (1, 1, SUBLANES, 128)
    scratch = [
        pltpu.VMEM((bb, N_HEADS, HEAD_DIM, HEAD_DIM), _F32),
        pltpu.VMEM((TAIL + tt, bb, D_B) if time_major_conv
                   else (bb, N_LANE_SLABS, XROW0 + tt, LANES), _F32),
        pltpu.VMEM((rows, D_A), _F32),
        pltpu.VMEM((N_LANE_SLABS, rows, LANES) if time_major_conv else (rows, D_B), _F32),
        pltpu.VMEM(hall_shape, _F32),
    ]
    h_out, s_out, c_out = pl.pallas_call(
        functools.partial(_layer_kernel, layer0, n_layers, bb, tt, chunk, n_tiles),
        grid=(n_layers, n_groups, n_tiles),
        in_specs=in_specs,
        out_specs=out_specs,
        out_shape=out_shape,
        scratch_shapes=scratch,
        compiler_params=pltpu.CompilerParams(
            dimension_semantics=("arbitrary", "arbitrary", "arbitrary"),
            vmem_limit_bytes=VMEM_LIMIT_BYTES),
    )(h, s0, c0, amat, lb_logits, norm_pre, norm_post, w_in, head_norm, conv_w, conv_b,
      ln_g, ln_b, w_pw2, b_pw2, w_out)
    if time_major_conv:
        c_out = jnp.swapaxes(c_out, 1, 2)
    return h_out, s_out, c_out


def kernel(x_prompt, x_sample, state_hgrn, state_conv, meta_tokens, norm_pre, norm_post, w_in,
           lb_logits, head_norm, conv_w, conv_b, conv_ln_g, conv_ln_b, w_pw2, b_pw2, w_out):
    depth = w_in.shape[0]
    row3 = lambda a: a.reshape(depth, 1, a.shape[-1])
    params = (lb_logits, row3(norm_pre), row3(norm_post), w_in.astype(_BF16), row3(head_norm),
              conv_w, row3(conv_b), row3(conv_ln_g), row3(conv_ln_b), w_pw2.astype(_BF16),
              row3(b_pw2), w_out.astype(_BF16))
    dtype = x_prompt.dtype

    n_dec = x_sample.shape[1]
    y_sample, s_sample, c_sample = _mixer_layers(
        0, depth, x_sample, state_hgrn, state_conv, params, bb=16, tt=n_dec, chunk=n_dec)

    hm = meta_tokens[None].astype(dtype)
    zero_s = jnp.zeros((depth, 1, N_HEADS, HEAD_DIM, HEAD_DIM), dtype)
    zero_c = jnp.zeros((depth, 1, TAIL, D_B), dtype)
    _, s_meta, c_meta = _mixer_layers(0, depth, hm, zero_s, zero_c, params, bb=1, tt=N_META,
                                      chunk=N_META)
    hp = x_prompt
    sp_l, cp_l = [], []
    for l in range(depth):
        hp, sp, cp = _mixer_layers(l, 1, hp, s_meta, c_meta, params, bb=1, tt=PROMPT_TILE,
                                   chunk=PROMPT_CHUNK, state_layer0=l)
        sp_l.append(sp)
        cp_l.append(cp)
    return (hp, y_sample, jnp.concatenate(sp_l), jnp.concatenate(cp_l), s_sample, c_sample)
```

```python
import functools

import numpy as np
import jax
import jax.numpy as jnp
from jax import lax
from jax.experimental import pallas as pl
from jax.experimental.pallas import tpu as pltpu

D_MODEL = 1024
D_A = 512
HEAD_DIM = 128
N_HEADS = D_A // HEAD_DIM
D_B = 512
CONV_WIDTH = 31
TAIL = CONV_WIDTH - 1
N_META = 16
D_IN = 4 * D_A + 3 * D_B
RMS_EPS = 1e-6
LN_EPS = 1e-5
LB_FLOOR = 1e-30
NEG_LOG2_E = -1.4426950408889634

SUBLANES = 8
LANES = 128
BF16_ROWS = 16
N_LANE_SLABS = D_B // LANES
XROW0 = 32
TAIL_ROW0 = XROW0 - TAIL

VMEM_LIMIT_BYTES = 56 * 1024 * 1024
PROMPT_TILE = 512
PROMPT_CHUNK = 64
SAMPLE_ROWS = 16
Z_GROUP = D_A
assert D_A == D_B and D_IN == 7 * Z_GROUP

_F32 = jnp.float32
_BF16 = jnp.bfloat16


def _levels(chunk):
    out = []
    m = chunk // 2
    while m >= 1:
        out.append(m)
        m //= 2
    return tuple(out)


def _decay_arg_matrix(chunk):
    blocks = []
    for m in _levels(chunk):
        a = np.zeros((chunk, chunk), np.float32)
        for r in range(chunk):
            mid = (r // (2 * m)) * 2 * m + m - 1
            if r > mid:
                a[r, mid + 1:r + 1] = 1.0
            else:
                a[r, r + 1:mid + 1] = 1.0
        blocks.append(a)
    blocks.append(np.tril(np.ones((chunk, chunk), np.float32)))
    blocks.append(np.triu(np.ones((chunk, chunk), np.float32), 1))
    return np.concatenate(blocks, axis=0)


def _sigmoid(x):
    return 0.5 * jnp.tanh(0.5 * x) + 0.5


def _silu(x):
    h = 0.5 * x
    return h * jnp.tanh(h) + h


def _dot(a, b):
    return jnp.dot(a, b, preferred_element_type=_F32)


def _dot_nt(a, b):
    return lax.dot_general(a, b, (((1,), (1,)), ((), ())), preferred_element_type=_F32)


def _dot_tn(a, b):
    return lax.dot_general(a, b, (((0,), (0,)), ((), ())), preferred_element_type=_F32)


def _split3(x):
    hi = x.astype(_BF16)
    r1 = x - hi.astype(_F32)
    mid = r1.astype(_BF16)
    lo = (r1 - mid.astype(_F32)).astype(_BF16)
    return hi, mid, lo


def _layer_kernel(layer0, n_layers, bb, tt, chunk, n_tiles,
                  h_ref, s0_ref, c0_ref, amat_ref, lbl_ref, gpre_ref, gpost_ref, win_ref,
                  ghead_ref, convw_ref, convb_ref, lng_ref, lnb_ref, wpw2_ref, bpw2_ref,
                  wout_ref, ho_ref, so_ref, co_ref, st_ref, xbuf_ref, oa_ref, y_ref, hall_ref):
    li = pl.program_id(0)
    gi = pl.program_id(1)
    j = pl.program_id(2)
    layer = layer0 if n_layers == 1 else layer0 + li
    rows = bb * tt
    n_chunks = tt // chunk
    levels = _levels(chunk)
    nlev = len(levels)

    single_tile = n_tiles == 1
    time_major_conv = single_tile and tt == SUBLANES and bb % SUBLANES == 0

    def load_state():
        for b in range(bb):
            for hh in range(N_HEADS):
                st_ref[b, hh] = s0_ref[b, hh].T
            if not time_major_conv:
                for cc in range(N_LANE_SLABS):
                    xbuf_ref[b, cc, pl.ds(TAIL_ROW0, TAIL), :] = c0_ref[b][:, cc * LANES:(cc + 1) * LANES]
        if time_major_conv:
            xbuf_ref[0:TAIL] = c0_ref[...]

    def store_state():
        for b in range(bb):
            for hh in range(N_HEADS):
                so_ref[b, hh] = st_ref[b, hh].T
            if not time_major_conv:
                for cc in range(N_LANE_SLABS):
                    co_ref[b, :, cc * LANES:(cc + 1) * LANES] = xbuf_ref[b, cc, pl.ds(TAIL_ROW0, TAIL), :]
        if time_major_conv:
            co_ref[...] = xbuf_ref[tt:tt + TAIL]

    if not single_tile:
        pl.when(j == 0)(load_state)

    lg = lbl_ref[...]
    pe = jnp.exp(lg - jnp.max(lg, axis=0, keepdims=True))
    p = pe / jnp.sum(pe, axis=0, keepdims=True)
    upto = lax.broadcasted_iota(jnp.int32, lg.shape, 0) <= layer
    lb = jnp.sum(jnp.where(upto, p, 0.0), axis=0, keepdims=True) - p[0:1, :]
    lb_floor = jnp.maximum(lb, LB_FLOOR)
    one_minus_lb = 1.0 - lb

    if n_layers == 1:
        x = h_ref[...].reshape(rows, D_MODEL)
    else:
        @pl.when(li == 0)
        def _first_layer_input():
            hall_ref[gi, j] = h_ref[...].reshape(rows, D_MODEL)
        x = hall_ref[gi, j]
    if single_tile:
        load_state()
    u = x * lax.rsqrt(jnp.mean(x * x, axis=-1, keepdims=True) + RMS_EPS) * gpre_ref[...]
    u = u.astype(_BF16)

    def zpart(i):
        return _dot(u, win_ref[:, i * Z_GROUP:(i + 1) * Z_GROUP])

    a = zpart(4) * _sigmoid(zpart(5))
    convw = convw_ref[...]
    convb = convb_ref[...]
    if time_major_conv:
        for cc in range(N_LANE_SLABS):
            y_ref[cc] = a[:, cc * LANES:(cc + 1) * LANES]
        for t in range(tt):
            for cc in range(N_LANE_SLABS):
                xbuf_ref[TAIL + t, :, cc * LANES:(cc + 1) * LANES] = y_ref[cc, pl.ds(t, bb, stride=tt), :]
        for t in range(tt):
            acc = None
            for tap in range(CONV_WIDTH):
                term = xbuf_ref[t + tap] * convw[tap:tap + 1, :]
                acc = term if acc is None else acc + term
            acc = acc + convb
            for cc in range(N_LANE_SLABS):
                y_ref[cc, pl.ds(t, bb, stride=tt), :] = acc[:, cc * LANES:(cc + 1) * LANES]
    else:
        for b in range(bb):
            for cc in range(N_LANE_SLABS):
                xbuf_ref[b, cc, pl.ds(XROW0, tt), :] = a[b * tt:(b + 1) * tt, cc * LANES:(cc + 1) * LANES]

    def conv_rows(b, t0, nt):
        for cc in range(N_LANE_SLABS):
            ls = slice(cc * LANES, (cc + 1) * LANES)
            acc = None
            for tap in range(CONV_WIDTH):
                term = xbuf_ref[b, cc, pl.ds(TAIL_ROW0 + t0 + tap, nt), :] * convw[tap:tap + 1, ls]
                acc = term if acc is None else acc + term
            y_ref[b * tt + t0:b * tt + t0 + nt, ls] = acc + convb[:, ls]

    q_all = zpart(0).astype(_BF16)
    zf = zpart(1)
    v_all = zpart(2).astype(_BF16)
    t = jnp.exp2(jnp.abs(zf) * NEG_LOG2_E)
    r = 1.0 / (1.0 + t)
    tr = t * r
    pos = zf >= 0
    sig = jnp.where(pos, r, tr)
    sneg = jnp.where(pos, tr, r)
    k_all = (one_minus_lb * sneg).astype(_BF16)
    logf = jnp.log2(sig + lb_floor * sneg)
    parts = _split3(logf)

    ti = lax.broadcasted_iota(jnp.int32, (chunk, chunk), 0)
    si = lax.broadcasted_iota(jnp.int32, (chunk, chunk), 1)
    ri = lax.broadcasted_iota(jnp.int32, (chunk, HEAD_DIM), 0)
    pair_masks = []
    second_half = []
    for m in levels:
        sh = int(np.log2(2 * m))
        same = (ti >> sh) == (si >> sh)
        pair_masks.append(same & ((ti & (2 * m - 1)) >= m) & ((si & (2 * m - 1)) < m))
        second_half.append((ri & (2 * m - 1)) >= m)
    diag_mask = ti == si
    amat = amat_ref[...]
    amat3 = jnp.concatenate([amat] * 3, axis=1)
    chunk_ids = [(b, c) for b in range(bb) for c in range(n_chunks)]
    heads = [slice(hh * HEAD_DIM, (hh + 1) * HEAD_DIM) for hh in range(N_HEADS)]

    e_all = {}
    for (b, c) in chunk_ids:
        rs = slice(b * tt + c * chunk, b * tt + (c + 1) * chunk)
        if chunk % BF16_ROWS == 0:
            args = _dot(amat3, jnp.concatenate([pt[rs] for pt in parts], axis=0))
        else:
            args = _dot(amat, parts[0][rs]) + _dot(amat, parts[1][rs]) + _dot(amat, parts[2][rs])
        e_all[b, c] = jnp.exp2(args)

    scores = {}
    for (b, c) in chunk_ids:
        rs = slice(b * tt + c * chunk, b * tt + (c + 1) * chunk)
        for hh, hs in enumerate(heads):
            qh = q_all[rs, hs]
            kh = k_all[rs, hs]
            eh = e_all[b, c][:, hs]
            prods = []
            for lv in range(nlev):
                em = eh[lv * chunk:(lv + 1) * chunk].astype(_BF16)
                xm = jnp.where(second_half[lv], qh, kh) * em
                prods.append(_dot_nt(xm, xm))
            pm = jnp.where(diag_mask, _dot_nt(qh, kh), 0.0)
            for lv in range(nlev):
                pm = jnp.where(pair_masks[lv], prods[lv], pm)
            scores[b, c, hh] = pm.astype(_BF16)
        if not time_major_conv:
            conv_rows(b, c * chunk, chunk)

    for b in range(bb):
        st = [st_ref[b, hh] for hh in range(N_HEADS)]
        for c in range(n_chunks):
            rs = slice(b * tt + c * chunk, b * tt + (c + 1) * chunk)
            for hh, hs in enumerate(heads):
                eh = e_all[b, c][:, hs]
                e_q = eh[nlev * chunk:(nlev + 1) * chunk]
                e_k = eh[(nlev + 1) * chunk:(nlev + 2) * chunk]
                qt = q_all[rs, hs] * e_q.astype(_BF16)
                kt = k_all[rs, hs] * e_k.astype(_BF16)
                vh = v_all[rs, hs]
                o = _dot(scores[b, c, hh], vh) + _dot_nt(qt, st[hh].astype(_BF16))
                st[hh] = st[hh] * e_q[chunk - 1:chunk, :] + _dot_tn(vh, kt)
                oa_ref[rs, hs] = o
        for hh in range(N_HEADS):
            st_ref[b, hh] = st[hh]

    if time_major_conv:
        y = jnp.concatenate([y_ref[cc] for cc in range(N_LANE_SLABS)], axis=1)
    else:
        for b in range(bb):
            for cc in range(N_LANE_SLABS):
                new_tail = xbuf_ref[b, cc, pl.ds(TAIL_ROW0 + tt, TAIL), :]
                xbuf_ref[b, cc, pl.ds(TAIL_ROW0, TAIL), :] = new_tail
        y = y_ref[...]
    mu = jnp.mean(y, axis=-1, keepdims=True)
    yc = y - mu
    y = yc * lax.rsqrt(jnp.mean(yc * yc, axis=-1, keepdims=True) + LN_EPS) * lng_ref[...] + lnb_ref[...]
    y = _silu(y)
    y = _dot(y.astype(_BF16), wpw2_ref[...]) + bpw2_ref[...]
    gb = zpart(6)
    y = (y * _silu(gb)).astype(_BF16)

    ga = zpart(3)
    ghead = ghead_ref[...]
    o_parts = []
    for hh, hs in enumerate(heads):
        oh = oa_ref[:, hs]
        oh = oh * lax.rsqrt(jnp.mean(oh * oh, axis=-1, keepdims=True) + RMS_EPS) * ghead
        gah = ga[:, hs]
        o_parts.append((oh * _silu(gah)).astype(_BF16))
    o_a = jnp.concatenate(o_parts, axis=-1)

    out = _dot(o_a, wout_ref[0:D_A, :]) + _dot(y, wout_ref[D_A:D_A + D_B, :])
    out = out * lax.rsqrt(jnp.mean(out * out, axis=-1, keepdims=True) + RMS_EPS) * gpost_ref[...]
    h_new = x + out
    if n_layers > 1:
        hall_ref[gi, j] = h_new
    ho_ref[...] = h_new.reshape(bb, tt, D_MODEL)

    if single_tile:
        store_state()
    else:
        pl.when(j == n_tiles - 1)(store_state)


def _mixer_layers(layer0, n_layers, h, s0, c0, params, *, bb, tt, chunk, state_layer0=0):
    (lb_logits, norm_pre, norm_post, w_in, head_norm, conv_w, conv_b, ln_g, ln_b,
     w_pw2, b_pw2, w_out) = params
    batch, seq, _ = h.shape
    assert batch % bb == 0 and seq % tt == 0 and tt % chunk == 0
    shared_state = s0.shape[1] == 1 and batch > 1
    assert not shared_state or bb == 1
    amat = jnp.asarray(_decay_arg_matrix(chunk), _BF16)
    rows = bb * tt
    n_groups, n_tiles = batch // bb, seq // tt
    time_major_conv = n_tiles == 1 and tt == SUBLANES and bb % SUBLANES == 0

    def state_idx(nd):
        if shared_state:
            return lambda l, i, j: (state_layer0 + l, 0) + (0,) * nd
        return lambda l, i, j: (state_layer0 + l, i) + (0,) * nd

    if time_major_conv:
        c0 = jnp.swapaxes(c0, 1, 2)
        tail_block = (None, TAIL, bb, D_B)
        tail_in_idx = lambda l, i, j: (state_layer0 + l, 0, i, 0)
        tail_out_idx = lambda l, i, j: (l, 0, i, 0)
        tail_out_shape = (n_layers, TAIL, batch, D_B)
    else:
        tail_block = (None, bb, TAIL, D_B)
        tail_in_idx = state_idx(2)
        tail_out_idx = lambda l, i, j: (l, i, 0, 0)
        tail_out_shape = (n_layers, batch, TAIL, D_B)

    def whole(arr):
        nd = arr.ndim
        return pl.BlockSpec(arr.shape, lambda l, i, j: (0,) * nd)

    def per_layer(arr):
        tail = arr.shape[1:]
        nz = len(tail)
        return pl.BlockSpec((None,) + tail, lambda l, i, j: (layer0 + l,) + (0,) * nz)

    state_block = (None, bb, N_HEADS, HEAD_DIM, HEAD_DIM)
    in_specs = [
        pl.BlockSpec((bb, tt, D_MODEL), lambda l, i, j: (i, j, 0)),
        pl.BlockSpec(state_block, state_idx(3)),
        pl.BlockSpec(tail_block, tail_in_idx),
        whole(amat),
        whole(lb_logits),
        per_layer(norm_pre), per_layer(norm_post), per_layer(w_in), per_layer(head_norm),
        per_layer(conv_w), per_layer(conv_b), per_layer(ln_g), per_layer(ln_b),
        per_layer(w_pw2), per_layer(b_pw2), per_layer(w_out),
    ]
    last = n_layers - 1

    def h_out_idx(l, i, j):
        if n_layers == 1:
            return (i, j, 0)
        return (jnp.where(l == last, i, 0), jnp.where(l == last, j, 0), 0)

    out_specs = [
        pl.BlockSpec((bb, tt, D_MODEL), h_out_idx),
        pl.BlockSpec(state_block, lambda l, i, j: (l, i, 0, 0, 0)),
        pl.BlockSpec(tail_block, tail_out_idx),
    ]
    out_shape = [
        jax.ShapeDtypeStruct((batch, seq, D_MODEL), h.dtype),
        jax.ShapeDtypeStruct((n_layers, batch, N_HEADS, HEAD_DIM, HEAD_DIM), h.dtype),
        jax.ShapeDtypeStruct(tail_out_shape, h.dtype),
    ]
    hall_shape = (n_groups, n_tiles, rows, D_MODEL) if n_layers > 1 else (1, 1, SUBLANES, LANES)
    scratch = [
        pltpu.VMEM((bb, N_HEADS, HEAD_DIM, HEAD_DIM), _F32),
        pltpu.VMEM((TAIL + tt, bb, D_B) if time_major_conv
                   else (bb, N_LANE_SLABS, XROW0 + tt, LANES), _F32),
        pltpu.VMEM((rows, D_A), _F32),
        pltpu.VMEM((N_LANE_SLABS, rows, LANES) if time_major_conv else (rows, D_B), _F32),
        pltpu.VMEM(hall_shape, _F32),
    ]
    h_out, s_out, c_out = pl.pallas_call(
        functools.partial(_layer_kernel, layer0, n_layers, bb, tt, chunk, n_tiles),
        grid=(n_layers, n_groups, n_tiles),
        in_specs=in_specs,
        out_specs=out_specs,
        out_shape=out_shape,
        scratch_shapes=scratch,
        compiler_params=pltpu.CompilerParams(
            dimension_semantics=("arbitrary", "arbitrary", "arbitrary"),
            vmem_limit_bytes=VMEM_LIMIT_BYTES),
    )(h, s0, c0, amat, lb_logits, norm_pre, norm_post, w_in, head_norm, conv_w, conv_b,
      ln_g, ln_b, w_pw2, b_pw2, w_out)
    if time_major_conv:
        c_out = jnp.swapaxes(c_out, 1, 2)
    return h_out, s_out, c_out


def kernel(x_prompt, x_sample, state_hgrn, state_conv, meta_tokens, norm_pre, norm_post, w_in,
           lb_logits, head_norm, conv_w, conv_b, conv_ln_g, conv_ln_b, w_pw2, b_pw2, w_out):
    depth = w_in.shape[0]
    row3 = lambda a: a.reshape(depth, 1, a.shape[-1])
    params = (lb_logits, row3(norm_pre), row3(norm_post), w_in.astype(_BF16), row3(head_norm),
              conv_w, row3(conv_b), row3(conv_ln_g), row3(conv_ln_b), w_pw2.astype(_BF16),
              row3(b_pw2), w_out.astype(_BF16))
    dtype = x_prompt.dtype

    n_dec = x_sample.shape[1]
    y_sample, s_sample, c_sample = _mixer_layers(
        0, depth, x_sample, state_hgrn, state_conv, params, bb=SAMPLE_ROWS, tt=n_dec, chunk=n_dec)

    hm = meta_tokens[None].astype(dtype)
    zero_s = jnp.zeros((depth, 1, N_HEADS, HEAD_DIM, HEAD_DIM), dtype)
    zero_c = jnp.zeros((depth, 1, TAIL, D_B), dtype)
    _, s_meta, c_meta = _mixer_layers(0, depth, hm, zero_s, zero_c, params, bb=1, tt=N_META,
                                      chunk=N_META)
    hp = x_prompt
    sp_l, cp_l = [], []
    for l in range(depth):
        hp, sp, cp = _mixer_layers(l, 1, hp, s_meta, c_meta, params, bb=1, tt=PROMPT_TILE,
                                   chunk=PROMPT_CHUNK, state_layer0=l)
        sp_l.append(sp)
        cp_l.append(cp)
    return (hp, y_sample, jnp.concatenate(sp_l), jnp.concatenate(cp_l), s_sample, c_sample)
```

```python
import functools

import numpy as np
import jax
import jax.numpy as jnp
from jax import lax
from jax.experimental import pallas as pl
from jax.experimental.pallas import tpu as pltpu

D_MODEL = 1024
D_A = 512
HEAD_DIM = 128
N_HEADS = D_A // HEAD_DIM
D_B = 512
CONV_WIDTH = 31
TAIL = CONV_WIDTH - 1
N_META = 16
D_IN = 4 * D_A + 3 * D_B
RMS_EPS = 1e-6
LN_EPS = 1e-5
LB_FLOOR = 1e-30
NEG_LOG2_E = -1.4426950408889634

SUBLANES = 8
LANES = 128
BF16_ROWS = 16
N_LANE_SLABS = D_B // LANES
XROW0 = 32
TAIL_ROW0 = XROW0 - TAIL

VMEM_LIMIT_BYTES = 56 * 1024 * 1024
PROMPT_TILE = 512
PROMPT_CHUNK = 64
PROJ_ROWS = 256
SAMPLE_ROWS = 16
Z_GROUP = D_A
assert D_A == D_B and D_IN == 7 * Z_GROUP

_F32 = jnp.float32
_BF16 = jnp.bfloat16


def _levels(chunk):
    out = []
    m = chunk // 2
    while m >= 1:
        out.append(m)
        m //= 2
    return tuple(out)


def _decay_arg_matrix(chunk):
    blocks = []
    for m in _levels(chunk):
        a = np.zeros((chunk, chunk), np.float32)
        for r in range(chunk):
            mid = (r // (2 * m)) * 2 * m + m - 1
            if r > mid:
                a[r, mid + 1:r + 1] = 1.0
            else:
                a[r, r + 1:mid + 1] = 1.0
        blocks.append(a)
    blocks.append(np.tril(np.ones((chunk, chunk), np.float32)))
    blocks.append(np.triu(np.ones((chunk, chunk), np.float32), 1))
    return np.concatenate(blocks, axis=0)


def _sigmoid(x):
    return 0.5 * jnp.tanh(0.5 * x) + 0.5


def _silu(x):
    h = 0.5 * x
    return h * jnp.tanh(h) + h


def _dot(a, b):
    return jnp.dot(a, b, preferred_element_type=_F32)


def _dot_nt(a, b):
    return lax.dot_general(a, b, (((1,), (1,)), ((), ())), preferred_element_type=_F32)


def _dot_tn(a, b):
    return lax.dot_general(a, b, (((0,), (0,)), ((), ())), preferred_element_type=_F32)


def _split3(x):
    hi = x.astype(_BF16)
    r1 = x - hi.astype(_F32)
    mid = r1.astype(_BF16)
    lo = (r1 - mid.astype(_F32)).astype(_BF16)
    return hi, mid, lo


def _layer_kernel(layer0, n_layers, bb, tt, chunk, n_tiles,
                  h_ref, s0_ref, c0_ref, amat_ref, lbl_ref, gpre_ref, gpost_ref, win_ref,
                  ghead_ref, convw_ref, convb_ref, lng_ref, lnb_ref, wpw2_ref, bpw2_ref,
                  wout_ref, ho_ref, so_ref, co_ref, st_ref, xbuf_ref, oa_ref, y_ref, hall_ref):
    li = pl.program_id(0)
    gi = pl.program_id(1)
    j = pl.program_id(2)
    layer = layer0 if n_layers == 1 else layer0 + li
    rows = bb * tt
    n_chunks = tt // chunk
    levels = _levels(chunk)
    nlev = len(levels)

    single_tile = n_tiles == 1
    time_major_conv = single_tile and tt == SUBLANES and bb % SUBLANES == 0

    def load_state():
        for b in range(bb):
            for hh in range(N_HEADS):
                st_ref[b, hh] = s0_ref[b, hh].T
            if not time_major_conv:
                for cc in range(N_LANE_SLABS):
                    xbuf_ref[b, cc, pl.ds(TAIL_ROW0, TAIL), :] = c0_ref[b][:, cc * LANES:(cc + 1) * LANES]
        if time_major_conv:
            xbuf_ref[0:TAIL] = c0_ref[...]

    def store_state():
        for b in range(bb):
            for hh in range(N_HEADS):
                so_ref[b, hh] = st_ref[b, hh].T
            if not time_major_conv:
                for cc in range(N_LANE_SLABS):
                    co_ref[b, :, cc * LANES:(cc + 1) * LANES] = xbuf_ref[b, cc, pl.ds(TAIL_ROW0, TAIL), :]
        if time_major_conv:
            co_ref[...] = xbuf_ref[tt:tt + TAIL]

    if not single_tile:
        pl.when(j == 0)(load_state)

    lg = lbl_ref[...]
    pe = jnp.exp(lg - jnp.max(lg, axis=0, keepdims=True))
    p = pe / jnp.sum(pe, axis=0, keepdims=True)
    upto = lax.broadcasted_iota(jnp.int32, lg.shape, 0) <= layer
    lb = jnp.sum(jnp.where(upto, p, 0.0), axis=0, keepdims=True) - p[0:1, :]
    lb_floor = jnp.maximum(lb, LB_FLOOR)
    one_minus_lb = 1.0 - lb

    if n_layers == 1:
        x = h_ref[...].reshape(rows, D_MODEL)
    else:
        @pl.when(li == 0)
        def _first_layer_input():
            hall_ref[gi, j] = h_ref[...].reshape(rows, D_MODEL)
        x = hall_ref[gi, j]
    if single_tile:
        load_state()
    u = x * lax.rsqrt(jnp.mean(x * x, axis=-1, keepdims=True) + RMS_EPS) * gpre_ref[...]
    u = u.astype(_BF16)

    n_blocks = max(1, rows // PROJ_ROWS)
    blk = rows // n_blocks
    z_blocks = {}
    for rb in range(n_blocks):
        u_rb = u[rb * blk:(rb + 1) * blk]
        for i in (4, 5, 1, 0, 2, 3, 6):
            z_blocks[rb, i] = _dot(u_rb, win_ref[:, i * Z_GROUP:(i + 1) * Z_GROUP])

    def zpart(i):
        return jnp.concatenate([z_blocks[rb, i] for rb in range(n_blocks)], axis=0)

    a = zpart(4) * _sigmoid(zpart(5))
    convw = convw_ref[...]
    convb = convb_ref[...]
    if time_major_conv:
        for cc in range(N_LANE_SLABS):
            y_ref[cc] = a[:, cc * LANES:(cc + 1) * LANES]
        for t in range(tt):
            for cc in range(N_LANE_SLABS):
                xbuf_ref[TAIL + t, :, cc * LANES:(cc + 1) * LANES] = y_ref[cc, pl.ds(t, bb, stride=tt), :]
        for t in range(tt):
            acc = None
            for tap in range(CONV_WIDTH):
                term = xbuf_ref[t + tap] * convw[tap:tap + 1, :]
                acc = term if acc is None else acc + term
            acc = acc + convb
            for cc in range(N_LANE_SLABS):
                y_ref[cc, pl.ds(t, bb, stride=tt), :] = acc[:, cc * LANES:(cc + 1) * LANES]
    else:
        for b in range(bb):
            for cc in range(N_LANE_SLABS):
                xbuf_ref[b, cc, pl.ds(XROW0, tt), :] = a[b * tt:(b + 1) * tt, cc * LANES:(cc + 1) * LANES]

    def conv_rows(b, t0, nt):
        for cc in range(N_LANE_SLABS):
            ls = slice(cc * LANES, (cc + 1) * LANES)
            acc = None
            for tap in range(CONV_WIDTH):
                term = xbuf_ref[b, cc, pl.ds(TAIL_ROW0 + t0 + tap, nt), :] * convw[tap:tap + 1, ls]
                acc = term if acc is None else acc + term
            y_ref[b * tt + t0:b * tt + t0 + nt, ls] = acc + convb[:, ls]

    q_all = zpart(0).astype(_BF16)
    zf = zpart(1)
    v_all = zpart(2).astype(_BF16)
    t = jnp.exp2(jnp.abs(zf) * NEG_LOG2_E)
    r = 1.0 / (1.0 + t)
    tr = t * r
    pos = zf >= 0
    sig = jnp.where(pos, r, tr)
    sneg = jnp.where(pos, tr, r)
    k_all = (one_minus_lb * sneg).astype(_BF16)
    logf = jnp.log2(sig + lb_floor * sneg)
    parts = _split3(logf)

    ti = lax.broadcasted_iota(jnp.int32, (chunk, chunk), 0)
    si = lax.broadcasted_iota(jnp.int32, (chunk, chunk), 1)
    ri = lax.broadcasted_iota(jnp.int32, (chunk, HEAD_DIM), 0)
    pair_masks = []
    second_half = []
    for m in levels:
        sh = int(np.log2(2 * m))
        same = (ti >> sh) == (si >> sh)
        pair_masks.append(same & ((ti & (2 * m - 1)) >= m) & ((si & (2 * m - 1)) < m))
        second_half.append((ri & (2 * m - 1)) >= m)
    diag_mask = ti == si
    amat = amat_ref[...]
    amat3 = jnp.concatenate([amat] * 3, axis=1)
    chunk_ids = [(b, c) for b in range(bb) for c in range(n_chunks)]
    heads = [slice(hh * HEAD_DIM, (hh + 1) * HEAD_DIM) for hh in range(N_HEADS)]

    e_all = {}
    for (b, c) in chunk_ids:
        rs = slice(b * tt + c * chunk, b * tt + (c + 1) * chunk)
        if chunk % BF16_ROWS == 0:
            args = _dot(amat3, jnp.concatenate([pt[rs] for pt in parts], axis=0))
        else:
            args = _dot(amat, parts[0][rs]) + _dot(amat, parts[1][rs]) + _dot(amat, parts[2][rs])
        e_all[b, c] = jnp.exp2(args)

    scores = {}
    for (b, c) in chunk_ids:
        rs = slice(b * tt + c * chunk, b * tt + (c + 1) * chunk)
        for hh, hs in enumerate(heads):
            qh = q_all[rs, hs]
            kh = k_all[rs, hs]
            eh = e_all[b, c][:, hs]
            prods = []
            for lv in range(nlev):
                em = eh[lv * chunk:(lv + 1) * chunk].astype(_BF16)
                xm = jnp.where(second_half[lv], qh, kh) * em
                prods.append(_dot_nt(xm, xm))
            pm = jnp.where(diag_mask, _dot_nt(qh, kh), 0.0)
            for lv in range(nlev):
                pm = jnp.where(pair_masks[lv], prods[lv], pm)
            scores[b, c, hh] = pm.astype(_BF16)
        if not time_major_conv:
            conv_rows(b, c * chunk, chunk)

    for b in range(bb):
        st = [st_ref[b, hh] for hh in range(N_HEADS)]
        for c in range(n_chunks):
            rs = slice(b * tt + c * chunk, b * tt + (c + 1) * chunk)
            for hh, hs in enumerate(heads):
                eh = e_all[b, c][:, hs]
                e_q = eh[nlev * chunk:(nlev + 1) * chunk]
                e_k = eh[(nlev + 1) * chunk:(nlev + 2) * chunk]
                qt = q_all[rs, hs] * e_q.astype(_BF16)
                kt = k_all[rs, hs] * e_k.astype(_BF16)
                vh = v_all[rs, hs]
                o = _dot(scores[b, c, hh], vh) + _dot_nt(qt, st[hh].astype(_BF16))
                st[hh] = st[hh] * e_q[chunk - 1:chunk, :] + _dot_tn(vh, kt)
                oa_ref[rs, hs] = o
        for hh in range(N_HEADS):
            st_ref[b, hh] = st[hh]

    if time_major_conv:
        y = jnp.concatenate([y_ref[cc] for cc in range(N_LANE_SLABS)], axis=1)
    else:
        for b in range(bb):
            for cc in range(N_LANE_SLABS):
                new_tail = xbuf_ref[b, cc, pl.ds(TAIL_ROW0 + tt, TAIL), :]
                xbuf_ref[b, cc, pl.ds(TAIL_ROW0, TAIL), :] = new_tail
        y = y_ref[...]
    mu = jnp.mean(y, axis=-1, keepdims=True)
    yc = y - mu
    y = yc * lax.rsqrt(jnp.mean(yc * yc, axis=-1, keepdims=True) + LN_EPS) * lng_ref[...] + lnb_ref[...]
    y = _silu(y)
    y = _dot(y.astype(_BF16), wpw2_ref[...]) + bpw2_ref[...]
    gb = zpart(6)
    y = (y * _silu(gb)).astype(_BF16)

    ga = zpart(3)
    ghead = ghead_ref[...]
    o_parts = []
    for hh, hs in enumerate(heads):
        oh = oa_ref[:, hs]
        oh = oh * lax.rsqrt(jnp.mean(oh * oh, axis=-1, keepdims=True) + RMS_EPS) * ghead
        gah = ga[:, hs]
        o_parts.append((oh * _silu(gah)).astype(_BF16))
    o_a = jnp.concatenate(o_parts, axis=-1)

    out = _dot(o_a, wout_ref[0:D_A, :]) + _dot(y, wout_ref[D_A:D_A + D_B, :])
    out = out * lax.rsqrt(jnp.mean(out * out, axis=-1, keepdims=True) + RMS_EPS) * gpost_ref[...]
    h_new = x + out
    if n_layers > 1:
        hall_ref[gi, j] = h_new
    ho_ref[...] = h_new.reshape(bb, tt, D_MODEL)

    if single_tile:
        store_state()
    else:
        pl.when(j == n_tiles - 1)(store_state)


def _mixer_layers(layer0, n_layers, h, s0, c0, params, *, bb, tt, chunk, state_layer0=0):
    (lb_logits, norm_pre, norm_post, w_in, head_norm, conv_w, conv_b, ln_g, ln_b,
     w_pw2, b_pw2, w_out) = params
    batch, seq, _ = h.shape
    assert batch % bb == 0 and seq % tt == 0 and tt % chunk == 0
    shared_state = s0.shape[1] == 1 and batch > 1
    assert not shared_state or bb == 1
    amat = jnp.asarray(_decay_arg_matrix(chunk), _BF16)
    rows = bb * tt
    n_groups, n_tiles = batch // bb, seq // tt
    time_major_conv = n_tiles == 1 and tt == SUBLANES and bb % SUBLANES == 0

    def state_idx(nd):
        if shared_state:
            return lambda l, i, j: (state_layer0 + l, 0) + (0,) * nd
        return lambda l, i, j: (state_layer0 + l, i) + (0,) * nd

    if time_major_conv:
        c0 = jnp.swapaxes(c0, 1, 2)
        tail_block = (None, TAIL, bb, D_B)
        tail_in_idx = lambda l, i, j: (state_layer0 + l, 0, i, 0)
        tail_out_idx = lambda l, i, j: (l, 0, i, 0)
        tail_out_shape = (n_layers, TAIL, batch, D_B)
    else:
        tail_block = (None, bb, TAIL, D_B)
        tail_in_idx = state_idx(2)
        tail_out_idx = lambda l, i, j: (l, i, 0, 0)
        tail_out_shape = (n_layers, batch, TAIL, D_B)

    def whole(arr):
        nd = arr.ndim
        return pl.BlockSpec(arr.shape, lambda l, i, j: (0,) * nd)

    def per_layer(arr):
        tail = arr.shape[1:]
        nz = len(tail)
        return pl.BlockSpec((None,) + tail, lambda l, i, j: (layer0 + l,) + (0,) * nz)

    state_block = (None, bb, N_HEADS, HEAD_DIM, HEAD_DIM)
    in_specs = [
        pl.BlockSpec((bb, tt, D_MODEL), lambda l, i, j: (i, j, 0)),
        pl.BlockSpec(state_block, state_idx(3)),
        pl.BlockSpec(tail_block, tail_in_idx),
        whole(amat),
        whole(lb_logits),
        per_layer(norm_pre), per_layer(norm_post), per_layer(w_in), per_layer(head_norm),
        per_layer(conv_w), per_layer(conv_b), per_layer(ln_g), per_layer(ln_b),
        per_layer(w_pw2), per_layer(b_pw2), per_layer(w_out),
    ]
    last = n_layers - 1

    def h_out_idx(l, i, j):
        if n_layers == 1:
            return (i, j, 0)
        return (jnp.where(l == last, i, 0), jnp.where(l == last, j, 0), 0)

    out_specs = [
        pl.BlockSpec((bb, tt, D_MODEL), h_out_idx),
        pl.BlockSpec(state_block, lambda l, i, j: (l, i, 0, 0, 0)),
        pl.BlockSpec(tail_block, tail_out_idx),
    ]
    out_shape = [
        jax.ShapeDtypeStruct((batch, seq, D_MODEL), h.dtype),
        jax.ShapeDtypeStruct((n_layers, batch, N_HEADS, HEAD_DIM, HEAD_DIM), h.dtype),
        jax.ShapeDtypeStruct(tail_out_shape, h.dtype),
    ]
    hall_shape = (n_groups, n_tiles, rows, D_MODEL) if n_layers > 1 else (1, 1, SUBLANES, LANES)
    scratch = [
        pltpu.VMEM((bb, N_HEADS, HEAD_DIM, HEAD_DIM), _F32),
        pltpu.VMEM((TAIL + tt, bb, D_B) if time_major_conv
                   else (bb, N_LANE_SLABS, XROW0 + tt, LANES), _F32),
        pltpu.VMEM((rows, D_A), _F32),
        pltpu.VMEM((N_LANE_SLABS, rows, LANES) if time_major_conv else (rows, D_B), _F32),
        pltpu.VMEM(hall_shape, _F32),
    ]
    h_out, s_out, c_out = pl.pallas_call(
        functools.partial(_layer_kernel, layer0, n_layers, bb, tt, chunk, n_tiles),
        grid=(n_layers, n_groups, n_tiles),
        in_specs=in_specs,
        out_specs=out_specs,
        out_shape=out_shape,
        scratch_shapes=scratch,
        compiler_params=pltpu.CompilerParams(
            dimension_semantics=("arbitrary", "arbitrary", "arbitrary"),
            vmem_limit_bytes=VMEM_LIMIT_BYTES),
    )(h, s0, c0, amat, lb_logits, norm_pre, norm_post, w_in, head_norm, conv_w, conv_b,
      ln_g, ln_b, w_pw2, b_pw2, w_out)
    if time_major_conv:
        c_out = jnp.swapaxes(c_out, 1, 2)
    return h_out, s_out, c_out


def kernel(x_prompt, x_sample, state_hgrn, state_conv, meta_tokens, norm_pre, norm_post, w_in,
           lb_logits, head_norm, conv_w, conv_b, conv_ln_g, conv_ln_b, w_pw2, b_pw2, w_out):
    depth = w_in.shape[0]
    row3 = lambda a: a.reshape(depth, 1, a.shape[-1])
    params = (lb_logits, row3(norm_pre), row3(norm_post), w_in.astype(_BF16), row3(head_norm),
              conv_w, row3(conv_b), row3(conv_ln_g), row3(conv_ln_b), w_pw2.astype(_BF16),
              row3(b_pw2), w_out.astype(_BF16))
    dtype = x_prompt.dtype

    n_dec = x_sample.shape[1]
    y_sample, s_sample, c_sample = _mixer_layers(
        0, depth, x_sample, state_hgrn, state_conv, params, bb=SAMPLE_ROWS, tt=n_dec, chunk=n_dec)

    hm = meta_tokens[None].astype(dtype)
    zero_s = jnp.zeros((depth, 1, N_HEADS, HEAD_DIM, HEAD_DIM), dtype)
    zero_c = jnp.zeros((depth, 1, TAIL, D_B), dtype)
    _, s_meta, c_meta = _mixer_layers(0, depth, hm, zero_s, zero_c, params, bb=1, tt=N_META,
                                      chunk=N_META)
    hp = x_prompt
    sp_l, cp_l = [], []
    for l in range(depth):
        hp, sp, cp = _mixer_layers(l, 1, hp, s_meta, c_meta, params, bb=1, tt=PROMPT_TILE,
                                   chunk=PROMPT_CHUNK, state_layer0=l)
        sp_l.append(sp)
        cp_l.append(cp)
    return (hp, y_sample, jnp.concatenate(sp_l), jnp.concatenate(cp_l), s_sample, c_sample)
```

```python
import functools

import numpy as np
import jax
import jax.numpy as jnp
from jax import lax
from jax.experimental import pallas as pl
from jax.experimental.pallas import tpu as pltpu

D_MODEL = 1024
D_A = 512
HEAD_DIM = 128
N_HEADS = D_A // HEAD_DIM
D_B = 512
CONV_WIDTH = 31
TAIL = CONV_WIDTH - 1
N_META = 16
D_IN = 4 * D_A + 3 * D_B
RMS_EPS = 1e-6
LN_EPS = 1e-5
LB_FLOOR = 1e-30
NEG_LOG2_E = -1.4426950408889634

SUBLANES = 8
LANES = 128
BF16_ROWS = 16
N_LANE_SLABS = D_B // LANES
XROW0 = -(-TAIL // SUBLANES) * SUBLANES
TAIL_ROW0 = XROW0 - TAIL

VMEM_LIMIT_BYTES = 56 * 1024 * 1024
PROMPT_TILE = 512
PROMPT_CHUNK = 64
PROJ_ROWS = 256
SAMPLE_ROWS = 16
Z_GROUP = D_A
assert D_A == D_B and D_IN == 7 * Z_GROUP

_F32 = jnp.float32
_BF16 = jnp.bfloat16


def _levels(chunk):
    out = []
    m = chunk // 2
    while m >= 1:
        out.append(m)
        m //= 2
    return tuple(out)


def _decay_arg_matrix(chunk):
    blocks = []
    for m in _levels(chunk)[:-1]:
        a = np.zeros((chunk, chunk), np.float32)
        for r in range(chunk):
            mid = (r // (2 * m)) * 2 * m + m - 1
            if r > mid:
                a[r, mid + 1:r + 1] = 1.0
            else:
                a[r, r + 1:mid + 1] = 1.0
        blocks.append(a)
    blocks.append(np.tril(np.ones((chunk, chunk), np.float32)))
    return np.concatenate(blocks, axis=0)


def _sigmoid(x):
    return 0.5 * jnp.tanh(0.5 * x) + 0.5


def _silu(x):
    h = 0.5 * x
    return h * jnp.tanh(h) + h


def _dot(a, b):
    return jnp.dot(a, b, preferred_element_type=_F32)


def _dot_nt(a, b):
    return lax.dot_general(a, b, (((1,), (1,)), ((), ())), preferred_element_type=_F32)


def _dot_tn(a, b):
    return lax.dot_general(a, b, (((0,), (0,)), ((), ())), preferred_element_type=_F32)


def _split3(x):
    hi = x.astype(_BF16)
    r1 = x - hi.astype(_F32)
    mid = r1.astype(_BF16)
    lo = (r1 - mid.astype(_F32)).astype(_BF16)
    return hi, mid, lo


def _layer_kernel(layer0, n_layers, bb, tt, chunk, n_tiles,
                  h_ref, s0_ref, c0_ref, amat_ref, lbl_ref, gpre_ref, gpost_ref, win_ref,
                  ghead_ref, convw_ref, convb_ref, lng_ref, lnb_ref, wpw2_ref, bpw2_ref,
                  wout_ref, ho_ref, so_ref, co_ref, st_ref, xbuf_ref, oa_ref, y_ref, hall_ref):
    li = pl.program_id(0)
    gi = pl.program_id(1)
    j = pl.program_id(2)
    layer = layer0 if n_layers == 1 else layer0 + li
    rows = bb * tt
    n_chunks = tt // chunk
    levels = _levels(chunk)
    nlev = len(levels)

    single_tile = n_tiles == 1
    time_major_conv = single_tile and tt == SUBLANES and bb % SUBLANES == 0

    def load_state():
        for b in range(bb):
            for hh in range(N_HEADS):
                st_ref[b, hh] = s0_ref[b, hh].T
            if not time_major_conv:
                for cc in range(N_LANE_SLABS):
                    xbuf_ref[b, cc, pl.ds(TAIL_ROW0, TAIL), :] = c0_ref[b][:, cc * LANES:(cc + 1) * LANES]
        if time_major_conv:
            xbuf_ref[0:TAIL] = c0_ref[...]

    def store_state():
        for b in range(bb):
            for hh in range(N_HEADS):
                so_ref[b, hh] = st_ref[b, hh].T
            if not time_major_conv:
                for cc in range(N_LANE_SLABS):
                    co_ref[b, :, cc * LANES:(cc + 1) * LANES] = xbuf_ref[b, cc, pl.ds(TAIL_ROW0, TAIL), :]
        if time_major_conv:
            co_ref[...] = xbuf_ref[tt:tt + TAIL]

    if not single_tile:
        pl.when(j == 0)(load_state)

    lg = lbl_ref[...]
    pe = jnp.exp(lg - jnp.max(lg, axis=0, keepdims=True))
    p = pe / jnp.sum(pe, axis=0, keepdims=True)
    upto = lax.broadcasted_iota(jnp.int32, lg.shape, 0) <= layer
    lb = jnp.sum(jnp.where(upto, p, 0.0), axis=0, keepdims=True) - p[0:1, :]
    lb_floor = jnp.maximum(lb, LB_FLOOR)
    one_minus_lb = 1.0 - lb

    if n_layers == 1:
        x = h_ref[...].reshape(rows, D_MODEL)
    else:
        @pl.when(li == 0)
        def _first_layer_input():
            hall_ref[gi, j] = h_ref[...].reshape(rows, D_MODEL)
        x = hall_ref[gi, j]
    if single_tile:
        load_state()
    u = x * lax.rsqrt(jnp.mean(x * x, axis=-1, keepdims=True) + RMS_EPS) * gpre_ref[...]
    u = u.astype(_BF16)

    n_blocks = max(1, rows // PROJ_ROWS)
    blk = rows // n_blocks
    z_blocks = {}
    for rb in range(n_blocks):
        u_rb = u[rb * blk:(rb + 1) * blk]
        for i in (4, 5, 1, 0, 2, 3, 6):
            z_blocks[rb, i] = _dot(u_rb, win_ref[:, i * Z_GROUP:(i + 1) * Z_GROUP])

    def zpart(i):
        return jnp.concatenate([z_blocks[rb, i] for rb in range(n_blocks)], axis=0)

    a = zpart(4) * _sigmoid(zpart(5))
    convw = convw_ref[...]
    convb = convb_ref[...]
    if time_major_conv:
        for cc in range(N_LANE_SLABS):
            y_ref[cc] = a[:, cc * LANES:(cc + 1) * LANES]
        for t in range(tt):
            for cc in range(N_LANE_SLABS):
                xbuf_ref[TAIL + t, :, cc * LANES:(cc + 1) * LANES] = y_ref[cc, pl.ds(t, bb, stride=tt), :]
        for t in range(tt):
            acc = None
            for tap in range(CONV_WIDTH):
                term = xbuf_ref[t + tap] * convw[tap:tap + 1, :]
                acc = term if acc is None else acc + term
            acc = acc + convb
            for cc in range(N_LANE_SLABS):
                y_ref[cc, pl.ds(t, bb, stride=tt), :] = acc[:, cc * LANES:(cc + 1) * LANES]
    else:
        for b in range(bb):
            for cc in range(N_LANE_SLABS):
                xbuf_ref[b, cc, pl.ds(XROW0, tt), :] = a[b * tt:(b + 1) * tt, cc * LANES:(cc + 1) * LANES]

    def conv_rows(b, t0, nt):
        for cc in range(N_LANE_SLABS):
            ls = slice(cc * LANES, (cc + 1) * LANES)
            acc = None
            for tap in range(CONV_WIDTH):
                term = xbuf_ref[b, cc, pl.ds(TAIL_ROW0 + t0 + tap, nt), :] * convw[tap:tap + 1, ls]
                acc = term if acc is None else acc + term
            y_ref[b * tt + t0:b * tt + t0 + nt, ls] = acc + convb[:, ls]

    q_all = zpart(0).astype(_BF16)
    zf = zpart(1)
    v_all = zpart(2).astype(_BF16)
    t = jnp.exp2(jnp.abs(zf) * NEG_LOG2_E)
    r = 1.0 / (1.0 + t)
    tr = t * r
    pos = zf >= 0
    sig = jnp.where(pos, r, tr)
    sneg = jnp.where(pos, tr, r)
    k_all = (one_minus_lb * sneg).astype(_BF16)
    logf = jnp.log2(sig + lb_floor * sneg)
    parts = _split3(logf)

    ti = lax.broadcasted_iota(jnp.int32, (chunk, chunk), 0)
    si = lax.broadcasted_iota(jnp.int32, (chunk, chunk), 1)
    ri = lax.broadcasted_iota(jnp.int32, (chunk, HEAD_DIM), 0)
    pair_masks = []
    second_half = []
    for m in levels:
        sh = int(np.log2(2 * m))
        same = (ti >> sh) == (si >> sh)
        pair_masks.append(same & ((ti & (2 * m - 1)) >= m) & ((si & (2 * m - 1)) < m))
        second_half.append((ri & (2 * m - 1)) >= m)
    diag_mask = ti == si
    amat = amat_ref[...]
    amat3 = jnp.concatenate([amat] * 3, axis=1)
    chunk_ids = [(b, c) for b in range(bb) for c in range(n_chunks)]
    heads = [slice(hh * HEAD_DIM, (hh + 1) * HEAD_DIM) for hh in range(N_HEADS)]

    odd_row = (lax.broadcasted_iota(jnp.int32, (chunk, D_A), 0) & 1) == 1
    e_all = {}
    for (b, c) in chunk_ids:
        rs = slice(b * tt + c * chunk, b * tt + (c + 1) * chunk)
        if chunk % BF16_ROWS == 0:
            args = _dot(amat3, jnp.concatenate([pt[rs] for pt in parts], axis=0))
        else:
            args = _dot(amat, parts[0][rs]) + _dot(amat, parts[1][rs]) + _dot(amat, parts[2][rs])
        b_run = args[(nlev - 1) * chunk:nlev * chunk]
        lvl1 = jnp.where(odd_row, logf[rs], 0.0)
        to_end = b_run[chunk - 1:chunk, :] - b_run
        e_all[b, c] = jnp.exp2(jnp.concatenate([args[0:(nlev - 1) * chunk], lvl1, b_run, to_end], axis=0))

    scores = {}
    for (b, c) in chunk_ids:
        rs = slice(b * tt + c * chunk, b * tt + (c + 1) * chunk)
        for hh, hs in enumerate(heads):
            qh = q_all[rs, hs]
            kh = k_all[rs, hs]
            eh = e_all[b, c][:, hs]
            prods = []
            for lv in range(nlev):
                em = eh[lv * chunk:(lv + 1) * chunk].astype(_BF16)
                xm = jnp.where(second_half[lv], qh, kh) * em
                prods.append(_dot_nt(xm, xm))
            pm = jnp.where(diag_mask, _dot_nt(qh, kh), 0.0)
            for lv in range(nlev):
                pm = jnp.where(pair_masks[lv], prods[lv], pm)
            scores[b, c, hh] = pm.astype(_BF16)
        if not time_major_conv:
            conv_rows(b, c * chunk, chunk)

    for b in range(bb):
        st = [st_ref[b, hh] for hh in range(N_HEADS)]
        for c in range(n_chunks):
            rs = slice(b * tt + c * chunk, b * tt + (c + 1) * chunk)
            for hh, hs in enumerate(heads):
                eh = e_all[b, c][:, hs]
                e_q = eh[nlev * chunk:(nlev + 1) * chunk]
                e_k = eh[(nlev + 1) * chunk:(nlev + 2) * chunk]
                qt = q_all[rs, hs] * e_q.astype(_BF16)
                kt = k_all[rs, hs] * e_k.astype(_BF16)
                vh = v_all[rs, hs]
                o = _dot(scores[b, c, hh], vh) + _dot_nt(qt, st[hh].astype(_BF16))
                st[hh] = st[hh] * e_q[chunk - 1:chunk, :] + _dot_tn(vh, kt)
                oa_ref[rs, hs] = o
        for hh in range(N_HEADS):
            st_ref[b, hh] = st[hh]

    if time_major_conv:
        y = jnp.concatenate([y_ref[cc] for cc in range(N_LANE_SLABS)], axis=1)
    else:
        for b in range(bb):
            for cc in range(N_LANE_SLABS):
                new_tail = xbuf_ref[b, cc, pl.ds(TAIL_ROW0 + tt, TAIL), :]
                xbuf_ref[b, cc, pl.ds(TAIL_ROW0, TAIL), :] = new_tail
        y = y_ref[...]
    mu = jnp.mean(y, axis=-1, keepdims=True)
    yc = y - mu
    y = yc * lax.rsqrt(jnp.mean(yc * yc, axis=-1, keepdims=True) + LN_EPS) * lng_ref[...] + lnb_ref[...]
    y = _silu(y)
    y = _dot(y.astype(_BF16), wpw2_ref[...]) + bpw2_ref[...]
    gb = zpart(6)
    y = (y * _silu(gb)).astype(_BF16)

    ga = zpart(3)
    ghead = ghead_ref[...]
    o_parts = []
    for hh, hs in enumerate(heads):
        oh = oa_ref[:, hs]
        oh = oh * lax.rsqrt(jnp.mean(oh * oh, axis=-1, keepdims=True) + RMS_EPS) * ghead
        gah = ga[:, hs]
        o_parts.append((oh * _silu(gah)).astype(_BF16))
    o_a = jnp.concatenate(o_parts, axis=-1)

    out = _dot(o_a, wout_ref[0:D_A, :]) + _dot(y, wout_ref[D_A:D_A + D_B, :])
    out = out * lax.rsqrt(jnp.mean(out * out, axis=-1, keepdims=True) + RMS_EPS) * gpost_ref[...]
    h_new = x + out
    if n_layers > 1:
        hall_ref[gi, j] = h_new
    ho_ref[...] = h_new.reshape(bb, tt, D_MODEL)

    if single_tile:
        store_state()
    else:
        pl.when(j == n_tiles - 1)(store_state)


def _mixer_layers(layer0, n_layers, h, s0, c0, params, *, bb, tt, chunk, state_layer0=0):
    (lb_logits, norm_pre, norm_post, w_in, head_norm, conv_w, conv_b, ln_g, ln_b,
     w_pw2, b_pw2, w_out) = params
    batch, seq, _ = h.shape
    assert batch % bb == 0 and seq % tt == 0 and tt % chunk == 0
    shared_state = s0.shape[1] == 1 and batch > 1
    assert not shared_state or bb == 1
    amat = jnp.asarray(_decay_arg_matrix(chunk), _BF16)
    rows = bb * tt
    n_groups, n_tiles = batch // bb, seq // tt
    time_major_conv = n_tiles == 1 and tt == SUBLANES and bb % SUBLANES == 0

    def state_idx(nd):
        if shared_state:
            return lambda l, i, j: (state_layer0 + l, 0) + (0,) * nd
        return lambda l, i, j: (state_layer0 + l, i) + (0,) * nd

    if time_major_conv:
        c0 = jnp.swapaxes(c0, 1, 2)
        tail_block = (None, TAIL, bb, D_B)
        tail_in_idx = lambda l, i, j: (state_layer0 + l, 0, i, 0)
        tail_out_idx = lambda l, i, j: (l, 0, i, 0)
        tail_out_shape = (n_layers, TAIL, batch, D_B)
    else:
        tail_block = (None, bb, TAIL, D_B)
        tail_in_idx = state_idx(2)
        tail_out_idx = lambda l, i, j: (l, i, 0, 0)
        tail_out_shape = (n_layers, batch, TAIL, D_B)

    def whole(arr):
        nd = arr.ndim
        return pl.BlockSpec(arr.shape, lambda l, i, j: (0,) * nd)

    def per_layer(arr):
        tail = arr.shape[1:]
        nz = len(tail)
        return pl.BlockSpec((None,) + tail, lambda l, i, j: (layer0 + l,) + (0,) * nz)

    state_block = (None, bb, N_HEADS, HEAD_DIM, HEAD_DIM)
    in_specs = [
        pl.BlockSpec((bb, tt, D_MODEL), lambda l, i, j: (i, j, 0)),
        pl.BlockSpec(state_block, state_idx(3)),
        pl.BlockSpec(tail_block, tail_in_idx),
        whole(amat),
        whole(lb_logits),
        per_layer(norm_pre), per_layer(norm_post), per_layer(w_in), per_layer(head_norm),
        per_layer(conv_w), per_layer(conv_b), per_layer(ln_g), per_layer(ln_b),
        per_layer(w_pw2), per_layer(b_pw2), per_layer(w_out),
    ]
    last = n_layers - 1

    def h_out_idx(l, i, j):
        if n_layers == 1:
            return (i, j, 0)
        return (jnp.where(l == last, i, 0), jnp.where(l == last, j, 0), 0)

    out_specs = [
        pl.BlockSpec((bb, tt, D_MODEL), h_out_idx),
        pl.BlockSpec(state_block, lambda l, i, j: (l, i, 0, 0, 0)),
        pl.BlockSpec(tail_block, tail_out_idx),
    ]
    out_shape = [
        jax.ShapeDtypeStruct((batch, seq, D_MODEL), h.dtype),
        jax.ShapeDtypeStruct((n_layers, batch, N_HEADS, HEAD_DIM, HEAD_DIM), h.dtype),
        jax.ShapeDtypeStruct(tail_out_shape, h.dtype),
    ]
    hall_shape = (n_groups, n_tiles, rows, D_MODEL) if n_layers > 1 else (1, 1, SUBLANES, LANES)
    scratch = [
        pltpu.VMEM((bb, N_HEADS, HEAD_DIM, HEAD_DIM), _F32),
        pltpu.VMEM((TAIL + tt, bb, D_B) if time_major_conv
                   else (bb, N_LANE_SLABS, XROW0 + tt, LANES), _F32),
        pltpu.VMEM((rows, D_A), _F32),
        pltpu.VMEM((N_LANE_SLABS, rows, LANES) if time_major_conv else (rows, D_B), _F32),
        pltpu.VMEM(hall_shape, _F32),
    ]
    h_out, s_out, c_out = pl.pallas_call(
        functools.partial(_layer_kernel, layer0, n_layers, bb, tt, chunk, n_tiles),
        grid=(n_layers, n_groups, n_tiles),
        in_specs=in_specs,
        out_specs=out_specs,
        out_shape=out_shape,
        scratch_shapes=scratch,
        compiler_params=pltpu.CompilerParams(
            dimension_semantics=("arbitrary", "arbitrary", "arbitrary"),
            vmem_limit_bytes=VMEM_LIMIT_BYTES),
    )(h, s0, c0, amat, lb_logits, norm_pre, norm_post, w_in, head_norm, conv_w, conv_b,
      ln_g, ln_b, w_pw2, b_pw2, w_out)
    if time_major_conv:
        c_out = jnp.swapaxes(c_out, 1, 2)
    return h_out, s_out, c_out


def kernel(x_prompt, x_sample, state_hgrn, state_conv, meta_tokens, norm_pre, norm_post, w_in,
           lb_logits, head_norm, conv_w, conv_b, conv_ln_g, conv_ln_b, w_pw2, b_pw2, w_out):
    depth = w_in.shape[0]
    row3 = lambda a: a.reshape(depth, 1, a.shape[-1])
    params = (lb_logits, row3(norm_pre), row3(norm_post), w_in.astype(_BF16), row3(head_norm),
              conv_w, row3(conv_b), row3(conv_ln_g), row3(conv_ln_b), w_pw2.astype(_BF16),
              row3(b_pw2), w_out.astype(_BF16))
    dtype = x_prompt.dtype

    n_dec = x_sample.shape[1]
    y_sample, s_sample, c_sample = _mixer_layers(
        0, depth, x_sample, state_hgrn, state_conv, params, bb=SAMPLE_ROWS, tt=n_dec, chunk=n_dec)

    hm = meta_tokens[None].astype(dtype)
    zero_s = jnp.zeros((depth, 1, N_HEADS, HEAD_DIM, HEAD_DIM), dtype)
    zero_c = jnp.zeros((depth, 1, TAIL, D_B), dtype)
    _, s_meta, c_meta = _mixer_layers(0, depth, hm, zero_s, zero_c, params, bb=1, tt=N_META,
                                      chunk=N_META)
    hp = x_prompt
    sp_l, cp_l = [], []
    for l in range(depth):
        hp, sp, cp = _mixer_layers(l, 1, hp, s_meta, c_meta, params, bb=1, tt=PROMPT_TILE,
                                   chunk=PROMPT_CHUNK, state_layer0=l)
        sp_l.append(sp)
        cp_l.append(cp)
    return (hp, y_sample, jnp.concatenate(sp_l), jnp.concatenate(cp_l), s_sample, c_sample)
```

```python
import functools

import numpy as np
import jax
import jax.numpy as jnp
from jax import lax
from jax.experimental import pallas as pl
from jax.experimental.pallas import tpu as pltpu

D_MODEL = 1024
D_A = 512
HEAD_DIM = 128
N_HEADS = D_A // HEAD_DIM
D_B = 512
CONV_WIDTH = 31
TAIL = CONV_WIDTH - 1
N_META = 16
D_IN = 4 * D_A + 3 * D_B
RMS_EPS = 1e-6
LN_EPS = 1e-5
LB_FLOOR = 1e-30
NEG_LOG2_E = -1.4426950408889634

SUBLANES = 8
LANES = 128
BF16_ROWS = 16
N_LANE_SLABS = D_B // LANES
XROW0 = -(-TAIL // SUBLANES) * SUBLANES
TAIL_ROW0 = XROW0 - TAIL

VMEM_LIMIT_BYTES = 56 * 1024 * 1024
PROMPT_TILE = 512
PROMPT_CHUNK = 64
PROJ_ROWS = 256
SAMPLE_ROWS = 16
Z_GROUP = D_A
assert D_A == D_B and D_IN == 7 * Z_GROUP

_F32 = jnp.float32
_BF16 = jnp.bfloat16


def _levels(chunk):
    out = []
    m = chunk // 2
    while m >= 1:
        out.append(m)
        m //= 2
    return tuple(out)


def _decay_arg_matrix(chunk):
    blocks = []
    for m in _levels(chunk)[:-1]:
        a = np.zeros((chunk, chunk), np.float32)
        for r in range(chunk):
            mid = (r // (2 * m)) * 2 * m + m - 1
            if r > mid:
                a[r, mid + 1:r + 1] = 1.0
            else:
                a[r, r + 1:mid + 1] = 1.0
        blocks.append(a)
    blocks.append(np.tril(np.ones((chunk, chunk), np.float32)))
    return np.concatenate(blocks, axis=0)


def _sigmoid(x):
    return 0.5 * jnp.tanh(0.5 * x) + 0.5


def _silu(x):
    h = 0.5 * x
    return h * jnp.tanh(h) + h


def _dot(a, b):
    return jnp.dot(a, b, preferred_element_type=_F32)


def _dot_nt(a, b):
    return lax.dot_general(a, b, (((1,), (1,)), ((), ())), preferred_element_type=_F32)


def _dot_tn(a, b):
    return lax.dot_general(a, b, (((0,), (0,)), ((), ())), preferred_element_type=_F32)


def _split3(x):
    hi = x.astype(_BF16)
    r1 = x - hi.astype(_F32)
    mid = r1.astype(_BF16)
    lo = (r1 - mid.astype(_F32)).astype(_BF16)
    return hi, mid, lo


def _layer_kernel(layer0, n_layers, bb, tt, chunk, n_tiles,
                  h_ref, s0_ref, c0_ref, amat_ref, lbl_ref, gpre_ref, gpost_ref, win_ref,
                  ghead_ref, convw_ref, convb_ref, lng_ref, lnb_ref, wpw2_ref, bpw2_ref,
                  wout_ref, ho_ref, so_ref, co_ref, st_ref, xbuf_ref, oa_ref, y_ref, hall_ref):
    li = pl.program_id(0)
    gi = pl.program_id(1)
    j = pl.program_id(2)
    layer = layer0 if n_layers == 1 else layer0 + li
    rows = bb * tt
    n_chunks = tt // chunk
    levels = _levels(chunk)
    nlev = len(levels)

    single_tile = n_tiles == 1
    time_major_conv = single_tile and tt == SUBLANES and bb % SUBLANES == 0

    def load_state():
        for b in range(bb):
            for hh in range(N_HEADS):
                st_ref[b, hh] = s0_ref[b, hh].T
            if not time_major_conv:
                for cc in range(N_LANE_SLABS):
                    xbuf_ref[b, cc, pl.ds(TAIL_ROW0, TAIL), :] = c0_ref[b][:, cc * LANES:(cc + 1) * LANES]
        if time_major_conv:
            xbuf_ref[0:TAIL] = c0_ref[...]

    def store_state():
        for b in range(bb):
            for hh in range(N_HEADS):
                so_ref[b, hh] = st_ref[b, hh].T
            if not time_major_conv:
                for cc in range(N_LANE_SLABS):
                    co_ref[b, :, cc * LANES:(cc + 1) * LANES] = xbuf_ref[b, cc, pl.ds(TAIL_ROW0, TAIL), :]
        if time_major_conv:
            co_ref[...] = xbuf_ref[tt:tt + TAIL]

    if not single_tile:
        pl.when(j == 0)(load_state)

    lg = lbl_ref[...]
    pe = jnp.exp(lg - jnp.max(lg, axis=0, keepdims=True))
    p = pe / jnp.sum(pe, axis=0, keepdims=True)
    upto = lax.broadcasted_iota(jnp.int32, lg.shape, 0) <= layer
    lb = jnp.sum(jnp.where(upto, p, 0.0), axis=0, keepdims=True) - p[0:1, :]
    lb_floor = jnp.maximum(lb, LB_FLOOR)
    one_minus_lb = 1.0 - lb

    if n_layers == 1:
        x = h_ref[...].reshape(rows, D_MODEL)
    else:
        @pl.when(li == 0)
        def _first_layer_input():
            hall_ref[gi, j] = h_ref[...].reshape(rows, D_MODEL)
        x = hall_ref[gi, j]
    if single_tile:
        load_state()
    u = x * lax.rsqrt(jnp.mean(x * x, axis=-1, keepdims=True) + RMS_EPS) * gpre_ref[...]
    u = u.astype(_BF16)

    n_blocks = max(1, rows // PROJ_ROWS)
    blk = rows // n_blocks
    z_blocks = {}
    for rb in range(n_blocks):
        u_rb = u[rb * blk:(rb + 1) * blk]
        for i in (4, 5, 1, 0, 2, 3, 6):
            z_blocks[rb, i] = _dot(u_rb, win_ref[:, i * Z_GROUP:(i + 1) * Z_GROUP])

    def zpart(i):
        return jnp.concatenate([z_blocks[rb, i] for rb in range(n_blocks)], axis=0)

    a = zpart(4) * _sigmoid(zpart(5))
    convw = convw_ref[...]
    convb = convb_ref[...]
    if time_major_conv:
        for cc in range(N_LANE_SLABS):
            y_ref[cc] = a[:, cc * LANES:(cc + 1) * LANES]
        for t in range(tt):
            for cc in range(N_LANE_SLABS):
                xbuf_ref[TAIL + t, :, cc * LANES:(cc + 1) * LANES] = y_ref[cc, pl.ds(t, bb, stride=tt), :]
        for t in range(tt):
            acc = None
            for tap in range(CONV_WIDTH):
                term = xbuf_ref[t + tap] * convw[tap:tap + 1, :]
                acc = term if acc is None else acc + term
            acc = acc + convb
            for cc in range(N_LANE_SLABS):
                y_ref[cc, pl.ds(t, bb, stride=tt), :] = acc[:, cc * LANES:(cc + 1) * LANES]
    else:
        for b in range(bb):
            for cc in range(N_LANE_SLABS):
                xbuf_ref[b, cc, pl.ds(XROW0, tt), :] = a[b * tt:(b + 1) * tt, cc * LANES:(cc + 1) * LANES]

    def conv_rows(b, t0, nt):
        for cc in range(N_LANE_SLABS):
            ls = slice(cc * LANES, (cc + 1) * LANES)
            acc = None
            for tap in range(CONV_WIDTH):
                term = xbuf_ref[b, cc, pl.ds(TAIL_ROW0 + t0 + tap, nt), :] * convw[tap:tap + 1, ls]
                acc = term if acc is None else acc + term
            y_ref[b * tt + t0:b * tt + t0 + nt, ls] = acc + convb[:, ls]

    q_all = zpart(0).astype(_BF16)
    zf = zpart(1)
    v_all = zpart(2).astype(_BF16)
    sig = 1.0 / (1.0 + jnp.exp2(zf * NEG_LOG2_E))
    sneg = 1.0 - sig
    k_all = (one_minus_lb * sneg).astype(_BF16)
    logf = jnp.log2(sig + lb_floor * sneg)
    parts = _split3(logf)

    ti = lax.broadcasted_iota(jnp.int32, (chunk, chunk), 0)
    si = lax.broadcasted_iota(jnp.int32, (chunk, chunk), 1)
    ri = lax.broadcasted_iota(jnp.int32, (chunk, HEAD_DIM), 0)
    pair_masks = []
    second_half = []
    for m in levels:
        sh = int(np.log2(2 * m))
        same = (ti >> sh) == (si >> sh)
        pair_masks.append(same & ((ti & (2 * m - 1)) >= m) & ((si & (2 * m - 1)) < m))
        second_half.append((ri & (2 * m - 1)) >= m)
    diag_mask = ti == si
    amat = amat_ref[...]
    amat3 = jnp.concatenate([amat] * 3, axis=1)
    chunk_ids = [(b, c) for b in range(bb) for c in range(n_chunks)]
    heads = [slice(hh * HEAD_DIM, (hh + 1) * HEAD_DIM) for hh in range(N_HEADS)]

    odd_row = (lax.broadcasted_iota(jnp.int32, (chunk, D_A), 0) & 1) == 1
    e_all = {}
    for (b, c) in chunk_ids:
        rs = slice(b * tt + c * chunk, b * tt + (c + 1) * chunk)
        if chunk % BF16_ROWS == 0:
            args = _dot(amat3, jnp.concatenate([pt[rs] for pt in parts], axis=0))
        else:
            args = _dot(amat, parts[0][rs]) + _dot(amat, parts[1][rs]) + _dot(amat, parts[2][rs])
        b_run = args[(nlev - 1) * chunk:nlev * chunk]
        lvl1 = jnp.where(odd_row, logf[rs], 0.0)
        to_end = b_run[chunk - 1:chunk, :] - b_run
        e_all[b, c] = jnp.exp2(jnp.concatenate([args[0:(nlev - 1) * chunk], lvl1, b_run, to_end], axis=0))

    scores = {}
    for (b, c) in chunk_ids:
        rs = slice(b * tt + c * chunk, b * tt + (c + 1) * chunk)
        for hh, hs in enumerate(heads):
            qh = q_all[rs, hs]
            kh = k_all[rs, hs]
            eh = e_all[b, c][:, hs]
            prods = []
            for lv in range(nlev):
                em = eh[lv * chunk:(lv + 1) * chunk].astype(_BF16)
                xm = jnp.where(second_half[lv], qh, kh) * em
                prods.append(_dot_nt(xm, xm))
            pm = jnp.where(diag_mask, _dot_nt(qh, kh), 0.0)
            for lv in range(nlev):
                pm = jnp.where(pair_masks[lv], prods[lv], pm)
            scores[b, c, hh] = pm.astype(_BF16)
        if not time_major_conv:
            conv_rows(b, c * chunk, chunk)

    for b in range(bb):
        st = [st_ref[b, hh] for hh in range(N_HEADS)]
        for c in range(n_chunks):
            rs = slice(b * tt + c * chunk, b * tt + (c + 1) * chunk)
            for hh, hs in enumerate(heads):
                eh = e_all[b, c][:, hs]
                e_q = eh[nlev * chunk:(nlev + 1) * chunk]
                e_k = eh[(nlev + 1) * chunk:(nlev + 2) * chunk]
                qt = q_all[rs, hs] * e_q.astype(_BF16)
                kt = k_all[rs, hs] * e_k.astype(_BF16)
                vh = v_all[rs, hs]
                o = _dot(scores[b, c, hh], vh) + _dot_nt(qt, st[hh].astype(_BF16))
                st[hh] = st[hh] * e_q[chunk - 1:chunk, :] + _dot_tn(vh, kt)
                oa_ref[rs, hs] = o
        for hh in range(N_HEADS):
            st_ref[b, hh] = st[hh]

    if time_major_conv:
        y = jnp.concatenate([y_ref[cc] for cc in range(N_LANE_SLABS)], axis=1)
    else:
        for b in range(bb):
            for cc in range(N_LANE_SLABS):
                new_tail = xbuf_ref[b, cc, pl.ds(TAIL_ROW0 + tt, TAIL), :]
                xbuf_ref[b, cc, pl.ds(TAIL_ROW0, TAIL), :] = new_tail
        y = y_ref[...]
    mu = jnp.mean(y, axis=-1, keepdims=True)
    yc = y - mu
    y = yc * lax.rsqrt(jnp.mean(yc * yc, axis=-1, keepdims=True) + LN_EPS) * lng_ref[...] + lnb_ref[...]
    y = _silu(y)
    y = _dot(y.astype(_BF16), wpw2_ref[...]) + bpw2_ref[...]
    gb = zpart(6)
    y = (y * _silu(gb)).astype(_BF16)

    ga = zpart(3)
    ghead = ghead_ref[...]
    o_parts = []
    for hh, hs in enumerate(heads):
        oh = oa_ref[:, hs]
        oh = oh * lax.rsqrt(jnp.mean(oh * oh, axis=-1, keepdims=True) + RMS_EPS) * ghead
        gah = ga[:, hs]
        o_parts.append((oh * _silu(gah)).astype(_BF16))
    o_a = jnp.concatenate(o_parts, axis=-1)

    out = _dot(jnp.concatenate([o_a, y], axis=-1), wout_ref[...])
    out = out * lax.rsqrt(jnp.mean(out * out, axis=-1, keepdims=True) + RMS_EPS) * gpost_ref[...]
    h_new = x + out
    if n_layers > 1:
        hall_ref[gi, j] = h_new
    ho_ref[...] = h_new.reshape(bb, tt, D_MODEL)

    if single_tile:
        store_state()
    else:
        pl.when(j == n_tiles - 1)(store_state)


def _mixer_layers(layer0, n_layers, h, s0, c0, params, *, bb, tt, chunk, state_layer0=0):
    (lb_logits, norm_pre, norm_post, w_in, head_norm, conv_w, conv_b, ln_g, ln_b,
     w_pw2, b_pw2, w_out) = params
    batch, seq, _ = h.shape
    assert batch % bb == 0 and seq % tt == 0 and tt % chunk == 0
    shared_state = s0.shape[1] == 1 and batch > 1
    assert not shared_state or bb == 1
    amat = jnp.asarray(_decay_arg_matrix(chunk), _BF16)
    rows = bb * tt
    n_groups, n_tiles = batch // bb, seq // tt
    time_major_conv = n_tiles == 1 and tt == SUBLANES and bb % SUBLANES == 0

    def state_idx(nd):
        if shared_state:
            return lambda l, i, j: (state_layer0 + l, 0) + (0,) * nd
        return lambda l, i, j: (state_layer0 + l, i) + (0,) * nd

    if time_major_conv:
        c0 = jnp.swapaxes(c0, 1, 2)
        tail_block = (None, TAIL, bb, D_B)
        tail_in_idx = lambda l, i, j: (state_layer0 + l, 0, i, 0)
        tail_out_idx = lambda l, i, j: (l, 0, i, 0)
        tail_out_shape = (n_layers, TAIL, batch, D_B)
    else:
        tail_block = (None, bb, TAIL, D_B)
        tail_in_idx = state_idx(2)
        tail_out_idx = lambda l, i, j: (l, i, 0, 0)
        tail_out_shape = (n_layers, batch, TAIL, D_B)

    def whole(arr):
        nd = arr.ndim
        return pl.BlockSpec(arr.shape, lambda l, i, j: (0,) * nd)

    def per_layer(arr):
        tail = arr.shape[1:]
        nz = len(tail)
        return pl.BlockSpec((None,) + tail, lambda l, i, j: (layer0 + l,) + (0,) * nz)

    state_block = (None, bb, N_HEADS, HEAD_DIM, HEAD_DIM)
    in_specs = [
        pl.BlockSpec((bb, tt, D_MODEL), lambda l, i, j: (i, j, 0)),
        pl.BlockSpec(state_block, state_idx(3)),
        pl.BlockSpec(tail_block, tail_in_idx),
        whole(amat),
        whole(lb_logits),
        per_layer(norm_pre), per_layer(norm_post), per_layer(w_in), per_layer(head_norm),
        per_layer(conv_w), per_layer(conv_b), per_layer(ln_g), per_layer(ln_b),
        per_layer(w_pw2), per_layer(b_pw2), per_layer(w_out),
    ]
    last = n_layers - 1

    def h_out_idx(l, i, j):
        if n_layers == 1:
            return (i, j, 0)
        return (jnp.where(l == last, i, 0), jnp.where(l == last, j, 0), 0)

    out_specs = [
        pl.BlockSpec((bb, tt, D_MODEL), h_out_idx),
        pl.BlockSpec(state_block, lambda l, i, j: (l, i, 0, 0, 0)),
        pl.BlockSpec(tail_block, tail_out_idx),
    ]
    out_shape = [
        jax.ShapeDtypeStruct((batch, seq, D_MODEL), h.dtype),
        jax.ShapeDtypeStruct((n_layers, batch, N_HEADS, HEAD_DIM, HEAD_DIM), h.dtype),
        jax.ShapeDtypeStruct(tail_out_shape, h.dtype),
    ]
    hall_shape = (n_groups, n_tiles, rows, D_MODEL) if n_layers > 1 else (1, 1, SUBLANES, LANES)
    scratch = [
        pltpu.VMEM((bb, N_HEADS, HEAD_DIM, HEAD_DIM), _F32),
        pltpu.VMEM((TAIL + tt, bb, D_B) if time_major_conv
                   else (bb, N_LANE_SLABS, XROW0 + tt, LANES), _F32),
        pltpu.VMEM((rows, D_A), _F32),
        pltpu.VMEM((N_LANE_SLABS, rows, LANES) if time_major_conv else (rows, D_B), _F32),
        pltpu.VMEM(hall_shape, _F32),
    ]
    h_out, s_out, c_out = pl.pallas_call(
        functools.partial(_layer_kernel, layer0, n_layers, bb, tt, chunk, n_tiles),
        grid=(n_layers, n_groups, n_tiles),
        in_specs=in_specs,
        out_specs=out_specs,
        out_shape=out_shape,
        scratch_shapes=scratch,
        compiler_params=pltpu.CompilerParams(
            dimension_semantics=("arbitrary", "arbitrary", "arbitrary"),
            vmem_limit_bytes=VMEM_LIMIT_BYTES),
    )(h, s0, c0, amat, lb_logits, norm_pre, norm_post, w_in, head_norm, conv_w, conv_b,
      ln_g, ln_b, w_pw2, b_pw2, w_out)
    if time_major_conv:
        c_out = jnp.swapaxes(c_out, 1, 2)
    return h_out, s_out, c_out


def kernel(x_prompt, x_sample, state_hgrn, state_conv, meta_tokens, norm_pre, norm_post, w_in,
           lb_logits, head_norm, conv_w, conv_b, conv_ln_g, conv_ln_b, w_pw2, b_pw2, w_out):
    depth = w_in.shape[0]
    row3 = lambda a: a.reshape(depth, 1, a.shape[-1])
    params = (lb_logits, row3(norm_pre), row3(norm_post), w_in.astype(_BF16), row3(head_norm),
              conv_w, row3(conv_b), row3(conv_ln_g), row3(conv_ln_b), w_pw2.astype(_BF16),
              row3(b_pw2), w_out.astype(_BF16))
    dtype = x_prompt.dtype

    n_dec = x_sample.shape[1]
    y_sample, s_sample, c_sample = _mixer_layers(
        0, depth, x_sample, state_hgrn, state_conv, params, bb=SAMPLE_ROWS, tt=n_dec, chunk=n_dec)

    hm = meta_tokens[None].astype(dtype)
    zero_s = jnp.zeros((depth, 1, N_HEADS, HEAD_DIM, HEAD_DIM), dtype)
    zero_c = jnp.zeros((depth, 1, TAIL, D_B), dtype)
    _, s_meta, c_meta = _mixer_layers(0, depth, hm, zero_s, zero_c, params, bb=1, tt=N_META,
                                      chunk=N_META)
    hp = x_prompt
    sp_l, cp_l = [], []
    for l in range(depth):
        hp, sp, cp = _mixer_layers(l, 1, hp, s_meta, c_meta, params, bb=1, tt=PROMPT_TILE,
                                   chunk=PROMPT_CHUNK, state_layer0=l)
        sp_l.append(sp)
        cp_l.append(cp)
    return (hp, y_sample, jnp.concatenate(sp_l), jnp.concatenate(cp_l), s_sample, c_sample)
```

```python
import functools

import numpy as np
import jax
import jax.numpy as jnp
from jax import lax
from jax.experimental import pallas as pl
from jax.experimental.pallas import tpu as pltpu

D_MODEL = 1024
D_A = 512
HEAD_DIM = 128
N_HEADS = D_A // HEAD_DIM
D_B = 512
CONV_WIDTH = 31
TAIL = CONV_WIDTH - 1
N_META = 16
D_IN = 4 * D_A + 3 * D_B
RMS_EPS = 1e-6
LN_EPS = 1e-5
LB_FLOOR = 1e-30
NEG_LOG2_E = -1.4426950408889634

SUBLANES = 8
LANES = 128
BF16_ROWS = 16
N_LANE_SLABS = D_B // LANES
XROW0 = -(-TAIL // SUBLANES) * SUBLANES
TAIL_ROW0 = XROW0 - TAIL

VMEM_LIMIT_BYTES = 56 * 1024 * 1024
PROMPT_TILE = 512
PROMPT_CHUNK = 64
PROJ_ROWS = 256
SAMPLE_ROWS = 16
Z_GROUP = D_A
assert D_A == D_B and D_IN == 7 * Z_GROUP

_F32 = jnp.float32
_BF16 = jnp.bfloat16


def _levels(chunk):
    out = []
    m = chunk // 2
    while m >= 1:
        out.append(m)
        m //= 2
    return tuple(out)


def _decay_arg_matrix(chunk):
    blocks = []
    for m in _levels(chunk)[:-1]:
        a = np.zeros((chunk, chunk), np.float32)
        for r in range(chunk):
            mid = (r // (2 * m)) * 2 * m + m - 1
            if r > mid:
                a[r, mid + 1:r + 1] = 1.0
            else:
                a[r, r + 1:mid + 1] = 1.0
        blocks.append(a)
    blocks.append(np.tril(np.ones((chunk, chunk), np.float32)))
    return np.concatenate(blocks, axis=0)


def _sigmoid(x):
    return 0.5 * jnp.tanh(0.5 * x) + 0.5


def _silu(x):
    h = 0.5 * x
    return h * jnp.tanh(h) + h


def _dot(a, b):
    return jnp.dot(a, b, preferred_element_type=_F32)


def _dot_nt(a, b):
    return lax.dot_general(a, b, (((1,), (1,)), ((), ())), preferred_element_type=_F32)


def _dot_tn(a, b):
    return lax.dot_general(a, b, (((0,), (0,)), ((), ())), preferred_element_type=_F32)


def _split3(x):
    hi = x.astype(_BF16)
    r1 = x - hi.astype(_F32)
    mid = r1.astype(_BF16)
    lo = (r1 - mid.astype(_F32)).astype(_BF16)
    return hi, mid, lo


def _layer_kernel(layer0, n_layers, bb, tt, chunk, n_tiles,
                  h_ref, s0_ref, c0_ref, amat_ref, lbl_ref, gpre_ref, gpost_ref, win_ref,
                  ghead_ref, convw_ref, convb_ref, lng_ref, lnb_ref, wpw2_ref, bpw2_ref,
                  wout_ref, ho_ref, so_ref, co_ref, st_ref, xbuf_ref, oa_ref, y_ref, hall_ref):
    li = pl.program_id(0)
    gi = pl.program_id(1)
    j = pl.program_id(2)
    layer = layer0 if n_layers == 1 else layer0 + li
    rows = bb * tt
    n_chunks = tt // chunk
    levels = _levels(chunk)
    nlev = len(levels)

    single_tile = n_tiles == 1
    time_major_conv = single_tile and tt == SUBLANES and bb % SUBLANES == 0

    def load_state():
        for b in range(bb):
            for hh in range(N_HEADS):
                st_ref[b, hh] = s0_ref[b, hh].T
            if not time_major_conv:
                for cc in range(N_LANE_SLABS):
                    xbuf_ref[b, cc, pl.ds(TAIL_ROW0, TAIL), :] = c0_ref[b][:, cc * LANES:(cc + 1) * LANES]
        if time_major_conv:
            xbuf_ref[0:TAIL] = c0_ref[...]

    def store_state():
        for b in range(bb):
            for hh in range(N_HEADS):
                so_ref[b, hh] = st_ref[b, hh].T
            if not time_major_conv:
                for cc in range(N_LANE_SLABS):
                    co_ref[b, :, cc * LANES:(cc + 1) * LANES] = xbuf_ref[b, cc, pl.ds(TAIL_ROW0, TAIL), :]
        if time_major_conv:
            co_ref[...] = xbuf_ref[tt:tt + TAIL]

    if not single_tile:
        pl.when(j == 0)(load_state)

    lg = lbl_ref[...]
    pe = jnp.exp(lg - jnp.max(lg, axis=0, keepdims=True))
    p = pe / jnp.sum(pe, axis=0, keepdims=True)
    upto = lax.broadcasted_iota(jnp.int32, lg.shape, 0) <= layer
    lb = jnp.sum(jnp.where(upto, p, 0.0), axis=0, keepdims=True) - p[0:1, :]
    lb_floor = jnp.maximum(lb, LB_FLOOR)
    one_minus_lb = 1.0 - lb

    if n_layers == 1:
        x = h_ref[...].reshape(rows, D_MODEL)
    else:
        @pl.when(li == 0)
        def _first_layer_input():
            hall_ref[gi, j] = h_ref[...].reshape(rows, D_MODEL)
        x = hall_ref[gi, j]
    if single_tile:
        load_state()
    u = x * lax.rsqrt(jnp.mean(x * x, axis=-1, keepdims=True) + RMS_EPS) * gpre_ref[...]
    u = u.astype(_BF16)

    n_blocks = max(1, rows // PROJ_ROWS)
    blk = rows // n_blocks
    z_blocks = {}
    for rb in range(n_blocks):
        u_rb = u[rb * blk:(rb + 1) * blk]
        if rows < PROJ_ROWS:
            z_rb = _dot(u_rb, win_ref[...])
            for i in range(D_IN // Z_GROUP):
                z_blocks[rb, i] = z_rb[:, i * Z_GROUP:(i + 1) * Z_GROUP]
            continue
        for i in (4, 5, 1, 0, 2, 3, 6):
            z_blocks[rb, i] = _dot(u_rb, win_ref[:, i * Z_GROUP:(i + 1) * Z_GROUP])

    def zpart(i):
        return jnp.concatenate([z_blocks[rb, i] for rb in range(n_blocks)], axis=0)

    a = zpart(4) * _sigmoid(zpart(5))
    convw = convw_ref[...]
    convb = convb_ref[...]
    if time_major_conv:
        for cc in range(N_LANE_SLABS):
            y_ref[cc] = a[:, cc * LANES:(cc + 1) * LANES]
        for t in range(tt):
            for cc in range(N_LANE_SLABS):
                xbuf_ref[TAIL + t, :, cc * LANES:(cc + 1) * LANES] = y_ref[cc, pl.ds(t, bb, stride=tt), :]
        for t in range(tt):
            acc = None
            for tap in range(CONV_WIDTH):
                term = xbuf_ref[t + tap] * convw[tap:tap + 1, :]
                acc = term if acc is None else acc + term
            acc = acc + convb
            for cc in range(N_LANE_SLABS):
                y_ref[cc, pl.ds(t, bb, stride=tt), :] = acc[:, cc * LANES:(cc + 1) * LANES]
    else:
        for b in range(bb):
            for cc in range(N_LANE_SLABS):
                xbuf_ref[b, cc, pl.ds(XROW0, tt), :] = a[b * tt:(b + 1) * tt, cc * LANES:(cc + 1) * LANES]

    def conv_rows(b, t0, nt):
        for cc in range(N_LANE_SLABS):
            ls = slice(cc * LANES, (cc + 1) * LANES)
            acc = None
            for tap in range(CONV_WIDTH):
                term = xbuf_ref[b, cc, pl.ds(TAIL_ROW0 + t0 + tap, nt), :] * convw[tap:tap + 1, ls]
                acc = term if acc is None else acc + term
            y_ref[b * tt + t0:b * tt + t0 + nt, ls] = acc + convb[:, ls]

    q_all = zpart(0).astype(_BF16)
    zf = zpart(1)
    v_all = zpart(2).astype(_BF16)
    sig = 1.0 / (1.0 + jnp.exp2(zf * NEG_LOG2_E))
    sneg = 1.0 - sig
    k_all = (one_minus_lb * sneg).astype(_BF16)
    logf = jnp.log2(sig + lb_floor * sneg)
    parts = _split3(logf)

    ti = lax.broadcasted_iota(jnp.int32, (chunk, chunk), 0)
    si = lax.broadcasted_iota(jnp.int32, (chunk, chunk), 1)
    ri = lax.broadcasted_iota(jnp.int32, (chunk, HEAD_DIM), 0)
    pair_masks = []
    second_half = []
    for m in levels:
        sh = int(np.log2(2 * m))
        same = (ti >> sh) == (si >> sh)
        pair_masks.append(same & ((ti & (2 * m - 1)) >= m) & ((si & (2 * m - 1)) < m))
        second_half.append((ri & (2 * m - 1)) >= m)
    diag_mask = ti == si
    amat = amat_ref[...]
    amat3 = jnp.concatenate([amat] * 3, axis=1)
    chunk_ids = [(b, c) for b in range(bb) for c in range(n_chunks)]
    heads = [slice(hh * HEAD_DIM, (hh + 1) * HEAD_DIM) for hh in range(N_HEADS)]

    odd_row = (lax.broadcasted_iota(jnp.int32, (chunk, D_A), 0) & 1) == 1
    e_all = {}
    for (b, c) in chunk_ids:
        rs = slice(b * tt + c * chunk, b * tt + (c + 1) * chunk)
        if chunk % BF16_ROWS == 0:
            args = _dot(amat3, jnp.concatenate([pt[rs] for pt in parts], axis=0))
        else:
            stacked_parts = jnp.concatenate([pt[rs].astype(_F32) for pt in parts], axis=0).astype(_BF16)
            args = _dot(amat3, stacked_parts)
        b_run = args[(nlev - 1) * chunk:nlev * chunk]
        lvl1 = jnp.where(odd_row, logf[rs], 0.0)
        to_end = b_run[chunk - 1:chunk, :] - b_run
        e_all[b, c] = jnp.exp2(jnp.concatenate([args[0:(nlev - 1) * chunk], lvl1, b_run, to_end], axis=0))

    scores = {}
    for (b, c) in chunk_ids:
        rs = slice(b * tt + c * chunk, b * tt + (c + 1) * chunk)
        for hh, hs in enumerate(heads):
            qh = q_all[rs, hs]
            kh = k_all[rs, hs]
            eh = e_all[b, c][:, hs]
            prods = []
            for lv in range(nlev):
                em = eh[lv * chunk:(lv + 1) * chunk].astype(_BF16)
                xm = jnp.where(second_half[lv], qh, kh) * em
                prods.append(_dot_nt(xm, xm))
            pm = jnp.where(diag_mask, _dot_nt(qh, kh), 0.0)
            for lv in range(nlev):
                pm = jnp.where(pair_masks[lv], prods[lv], pm)
            scores[b, c, hh] = pm.astype(_BF16)
        if not time_major_conv:
            conv_rows(b, c * chunk, chunk)

    for b in range(bb):
        st = [st_ref[b, hh] for hh in range(N_HEADS)]
        for c in range(n_chunks):
            rs = slice(b * tt + c * chunk, b * tt + (c + 1) * chunk)
            for hh, hs in enumerate(heads):
                eh = e_all[b, c][:, hs]
                e_q = eh[nlev * chunk:(nlev + 1) * chunk]
                e_k = eh[(nlev + 1) * chunk:(nlev + 2) * chunk]
                qt = q_all[rs, hs] * e_q.astype(_BF16)
                kt = k_all[rs, hs] * e_k.astype(_BF16)
                vh = v_all[rs, hs]
                o = _dot(scores[b, c, hh], vh) + _dot_nt(qt, st[hh].astype(_BF16))
                st[hh] = st[hh] * e_q[chunk - 1:chunk, :] + _dot_tn(vh, kt)
                oa_ref[rs, hs] = o
        for hh in range(N_HEADS):
            st_ref[b, hh] = st[hh]

    if time_major_conv:
        y = jnp.concatenate([y_ref[cc] for cc in range(N_LANE_SLABS)], axis=1)
    else:
        for b in range(bb):
            for cc in range(N_LANE_SLABS):
                new_tail = xbuf_ref[b, cc, pl.ds(TAIL_ROW0 + tt, TAIL), :]
                xbuf_ref[b, cc, pl.ds(TAIL_ROW0, TAIL), :] = new_tail
        y = y_ref[...]
    mu = jnp.mean(y, axis=-1, keepdims=True)
    yc = y - mu
    y = yc * lax.rsqrt(jnp.mean(yc * yc, axis=-1, keepdims=True) + LN_EPS) * lng_ref[...] + lnb_ref[...]
    y = _silu(y)
    y = _dot(y.astype(_BF16), wpw2_ref[...]) + bpw2_ref[...]
    gb = zpart(6)
    y = (y * _silu(gb)).astype(_BF16)

    ga = zpart(3)
    ghead = ghead_ref[...]
    o_parts = []
    for hh, hs in enumerate(heads):
        oh = oa_ref[:, hs]
        oh = oh * lax.rsqrt(jnp.mean(oh * oh, axis=-1, keepdims=True) + RMS_EPS) * ghead
        gah = ga[:, hs]
        o_parts.append((oh * _silu(gah)).astype(_BF16))
    o_a = jnp.concatenate(o_parts, axis=-1)

    out = _dot(jnp.concatenate([o_a, y], axis=-1), wout_ref[...])
    out = out * lax.rsqrt(jnp.mean(out * out, axis=-1, keepdims=True) + RMS_EPS) * gpost_ref[...]
    h_new = x + out
    if n_layers > 1:
        hall_ref[gi, j] = h_new
    ho_ref[...] = h_new.reshape(bb, tt, D_MODEL)

    if single_tile:
        store_state()
    else:
        pl.when(j == n_tiles - 1)(store_state)


def _mixer_layers(layer0, n_layers, h, s0, c0, params, *, bb, tt, chunk, state_layer0=0):
    (lb_logits, norm_pre, norm_post, w_in, head_norm, conv_w, conv_b, ln_g, ln_b,
     w_pw2, b_pw2, w_out) = params
    batch, seq, _ = h.shape
    assert batch % bb == 0 and seq % tt == 0 and tt % chunk == 0
    shared_state = s0.shape[1] == 1 and batch > 1
    assert not shared_state or bb == 1
    amat = jnp.asarray(_decay_arg_matrix(chunk), _BF16)
    rows = bb * tt
    n_groups, n_tiles = batch // bb, seq // tt
    time_major_conv = n_tiles == 1 and tt == SUBLANES and bb % SUBLANES == 0

    def state_idx(nd):
        if shared_state:
            return lambda l, i, j: (state_layer0 + l, 0) + (0,) * nd
        return lambda l, i, j: (state_layer0 + l, i) + (0,) * nd

    if time_major_conv:
        c0 = jnp.swapaxes(c0, 1, 2)
        tail_block = (None, TAIL, bb, D_B)
        tail_in_idx = lambda l, i, j: (state_layer0 + l, 0, i, 0)
        tail_out_idx = lambda l, i, j: (l, 0, i, 0)
        tail_out_shape = (n_layers, TAIL, batch, D_B)
    else:
        tail_block = (None, bb, TAIL, D_B)
        tail_in_idx = state_idx(2)
        tail_out_idx = lambda l, i, j: (l, i, 0, 0)
        tail_out_shape = (n_layers, batch, TAIL, D_B)

    def whole(arr):
        nd = arr.ndim
        return pl.BlockSpec(arr.shape, lambda l, i, j: (0,) * nd)

    def per_layer(arr):
        tail = arr.shape[1:]
        nz = len(tail)
        return pl.BlockSpec((None,) + tail, lambda l, i, j: (layer0 + l,) + (0,) * nz)

    state_block = (None, bb, N_HEADS, HEAD_DIM, HEAD_DIM)
    in_specs = [
        pl.BlockSpec((bb, tt, D_MODEL), lambda l, i, j: (i, j, 0)),
        pl.BlockSpec(state_block, state_idx(3)),
        pl.BlockSpec(tail_block, tail_in_idx),
        whole(amat),
        whole(lb_logits),
        per_layer(norm_pre), per_layer(norm_post), per_layer(w_in), per_layer(head_norm),
        per_layer(conv_w), per_layer(conv_b), per_layer(ln_g), per_layer(ln_b),
        per_layer(w_pw2), per_layer(b_pw2), per_layer(w_out),
    ]
    last = n_layers - 1

    def h_out_idx(l, i, j):
        if n_layers == 1:
            return (i, j, 0)
        return (jnp.where(l == last, i, 0), jnp.where(l == last, j, 0), 0)

    out_specs = [
        pl.BlockSpec((bb, tt, D_MODEL), h_out_idx),
        pl.BlockSpec(state_block, lambda l, i, j: (l, i, 0, 0, 0)),
        pl.BlockSpec(tail_block, tail_out_idx),
    ]
    out_shape = [
        jax.ShapeDtypeStruct((batch, seq, D_MODEL), h.dtype),
        jax.ShapeDtypeStruct((n_layers, batch, N_HEADS, HEAD_DIM, HEAD_DIM), h.dtype),
        jax.ShapeDtypeStruct(tail_out_shape, h.dtype),
    ]
    hall_shape = (n_groups, n_tiles, rows, D_MODEL) if n_layers > 1 else (1, 1, SUBLANES, LANES)
    scratch = [
        pltpu.VMEM((bb, N_HEADS, HEAD_DIM, HEAD_DIM), _F32),
        pltpu.VMEM((TAIL + tt, bb, D_B) if time_major_conv
                   else (bb, N_LANE_SLABS, XROW0 + tt, LANES), _F32),
        pltpu.VMEM((rows, D_A), _F32),
        pltpu.VMEM((N_LANE_SLABS, rows, LANES) if time_major_conv else (rows, D_B), _F32),
        pltpu.VMEM(hall_shape, _F32),
    ]
    h_out, s_out, c_out = pl.pallas_call(
        functools.partial(_layer_kernel, layer0, n_layers, bb, tt, chunk, n_tiles),
        grid=(n_layers, n_groups, n_tiles),
        in_specs=in_specs,
        out_specs=out_specs,
        out_shape=out_shape,
        scratch_shapes=scratch,
        compiler_params=pltpu.CompilerParams(
            dimension_semantics=("arbitrary", "arbitrary", "arbitrary"),
            vmem_limit_bytes=VMEM_LIMIT_BYTES),
    )(h, s0, c0, amat, lb_logits, norm_pre, norm_post, w_in, head_norm, conv_w, conv_b,
      ln_g, ln_b, w_pw2, b_pw2, w_out)
    if time_major_conv:
        c_out = jnp.swapaxes(c_out, 1, 2)
    return h_out, s_out, c_out


def kernel(x_prompt, x_sample, state_hgrn, state_conv, meta_tokens, norm_pre, norm_post, w_in,
           lb_logits, head_norm, conv_w, conv_b, conv_ln_g, conv_ln_b, w_pw2, b_pw2, w_out):
    depth = w_in.shape[0]
    row3 = lambda a: a.reshape(depth, 1, a.shape[-1])
    params = (lb_logits, row3(norm_pre), row3(norm_post), w_in.astype(_BF16), row3(head_norm),
              conv_w, row3(conv_b), row3(conv_ln_g), row3(conv_ln_b), w_pw2.astype(_BF16),
              row3(b_pw2), w_out.astype(_BF16))
    dtype = x_prompt.dtype

    n_dec = x_sample.shape[1]
    y_sample, s_sample, c_sample = _mixer_layers(
        0, depth, x_sample, state_hgrn, state_conv, params, bb=SAMPLE_ROWS, tt=n_dec, chunk=n_dec)

    hm = meta_tokens[None].astype(dtype)
    zero_s = jnp.zeros((depth, 1, N_HEADS, HEAD_DIM, HEAD_DIM), dtype)
    zero_c = jnp.zeros((depth, 1, TAIL, D_B), dtype)
    _, s_meta, c_meta = _mixer_layers(0, depth, hm, zero_s, zero_c, params, bb=1, tt=N_META,
                                      chunk=N_META)
    hp = x_prompt
    sp_l, cp_l = [], []
    for l in range(depth):
        hp, sp, cp = _mixer_layers(l, 1, hp, s_meta, c_meta, params, bb=1, tt=PROMPT_TILE,
                                   chunk=PROMPT_CHUNK, state_layer0=l)
        sp_l.append(sp)
        cp_l.append(cp)
    return (hp, y_sample, jnp.concatenate(sp_l), jnp.concatenate(cp_l), s_sample, c_sample)
```

```python
import functools

import numpy as np
import jax
import jax.numpy as jnp
from jax import lax
from jax.experimental import pallas as pl
from jax.experimental.pallas import tpu as pltpu

D_MODEL = 1024
D_A = 512
HEAD_DIM = 128
N_HEADS = D_A // HEAD_DIM
D_B = 512
CONV_WIDTH = 31
TAIL = CONV_WIDTH - 1
N_META = 16
D_IN = 4 * D_A + 3 * D_B
RMS_EPS = 1e-6
LN_EPS = 1e-5
LB_FLOOR = 1e-30
NEG_LOG2_E = -1.4426950408889634

SUBLANES = 8
LANES = 128
BF16_ROWS = 16
N_LANE_SLABS = D_B // LANES
XROW0 = -(-TAIL // SUBLANES) * SUBLANES
TAIL_ROW0 = XROW0 - TAIL

VMEM_LIMIT_BYTES = 56 * 1024 * 1024
PROMPT_TILE = 512
PROMPT_CHUNK = 64
PROJ_ROWS = 256
SAMPLE_ROWS = 16
Z_GROUP = D_A
assert D_A == D_B and D_IN == 7 * Z_GROUP

_F32 = jnp.float32
_BF16 = jnp.bfloat16


def _levels(chunk):
    out = []
    m = chunk // 2
    while m >= 1:
        out.append(m)
        m //= 2
    return tuple(out)


def _decay_arg_matrix(chunk):
    blocks = []
    for m in _levels(chunk)[:-1]:
        a = np.zeros((chunk, chunk), np.float32)
        for r in range(chunk):
            mid = (r // (2 * m)) * 2 * m + m - 1
            if r > mid:
                a[r, mid + 1:r + 1] = 1.0
            else:
                a[r, r + 1:mid + 1] = 1.0
        blocks.append(a)
    blocks.append(np.tril(np.ones((chunk, chunk), np.float32)))
    return np.concatenate(blocks, axis=0)


def _sigmoid(x):
    return 0.5 * jnp.tanh(0.5 * x) + 0.5


def _silu(x):
    h = 0.5 * x
    return h * jnp.tanh(h) + h


def _dot(a, b):
    return jnp.dot(a, b, preferred_element_type=_F32)


def _dot_nt(a, b):
    return lax.dot_general(a, b, (((1,), (1,)), ((), ())), preferred_element_type=_F32)


def _dot_tn(a, b):
    return lax.dot_general(a, b, (((0,), (0,)), ((), ())), preferred_element_type=_F32)


def _split3(x):
    hi = x.astype(_BF16)
    r1 = x - hi.astype(_F32)
    mid = r1.astype(_BF16)
    lo = (r1 - mid.astype(_F32)).astype(_BF16)
    return hi, mid, lo


def _layer_kernel(layer0, n_layers, bb, tt, chunk, n_tiles,
                  h_ref, s0_ref, c0_ref, amat_ref, lbl_ref, gpre_ref, gpost_ref, win_ref,
                  ghead_ref, convw_ref, convb_ref, lng_ref, lnb_ref, wpw2_ref, bpw2_ref,
                  wout_ref, ho_ref, so_ref, co_ref, st_ref, xbuf_ref, oa_ref, y_ref, hall_ref):
    li = pl.program_id(0)
    gi = pl.program_id(1)
    j = pl.program_id(2)
    layer = layer0 if n_layers == 1 else layer0 + li
    rows = bb * tt
    n_chunks = tt // chunk
    levels = _levels(chunk)
    nlev = len(levels)

    single_tile = n_tiles == 1
    time_major_conv = single_tile and tt == SUBLANES and bb % SUBLANES == 0

    def load_state():
        for b in range(bb):
            for hh in range(N_HEADS):
                st_ref[b, hh] = s0_ref[b, hh].T
            if not time_major_conv:
                for cc in range(N_LANE_SLABS):
                    xbuf_ref[b, cc, pl.ds(TAIL_ROW0, TAIL), :] = c0_ref[b][:, cc * LANES:(cc + 1) * LANES]
        if time_major_conv:
            xbuf_ref[0:TAIL] = c0_ref[...]

    def store_state():
        for b in range(bb):
            for hh in range(N_HEADS):
                so_ref[b, hh] = st_ref[b, hh].T
            if not time_major_conv:
                for cc in range(N_LANE_SLABS):
                    co_ref[b, :, cc * LANES:(cc + 1) * LANES] = xbuf_ref[b, cc, pl.ds(TAIL_ROW0, TAIL), :]
        if time_major_conv:
            co_ref[...] = xbuf_ref[tt:tt + TAIL]

    if not single_tile:
        pl.when(j == 0)(load_state)

    lg = lbl_ref[...]
    pe = jnp.exp(lg - jnp.max(lg, axis=0, keepdims=True))
    p = pe / jnp.sum(pe, axis=0, keepdims=True)
    upto = lax.broadcasted_iota(jnp.int32, lg.shape, 0) <= layer
    lb = jnp.sum(jnp.where(upto, p, 0.0), axis=0, keepdims=True) - p[0:1, :]
    lb_floor = jnp.maximum(lb, LB_FLOOR)
    one_minus_lb = 1.0 - lb

    if n_layers == 1:
        x = h_ref[...].reshape(rows, D_MODEL)
    else:
        @pl.when(li == 0)
        def _first_layer_input():
            hall_ref[gi, j] = h_ref[...].reshape(rows, D_MODEL)
        x = hall_ref[gi, j]
    if single_tile:
        load_state()
    u = x * lax.rsqrt(jnp.mean(x * x, axis=-1, keepdims=True) + RMS_EPS) * gpre_ref[...]
    u = u.astype(_BF16)

    n_blocks = max(1, rows // PROJ_ROWS)
    blk = rows // n_blocks
    z_blocks = {}
    for rb in range(n_blocks):
        u_rb = u[rb * blk:(rb + 1) * blk]
        if rows < PROJ_ROWS:
            z_rb = _dot(u_rb, win_ref[...])
            for i in range(D_IN // Z_GROUP):
                z_blocks[rb, i] = z_rb[:, i * Z_GROUP:(i + 1) * Z_GROUP]
            continue
        for i in (4, 5, 1, 0, 2):
            z_blocks[rb, i] = _dot(u_rb, win_ref[:, i * Z_GROUP:(i + 1) * Z_GROUP])
    for rb in range(n_blocks):
        if rows < PROJ_ROWS:
            continue
        for i in (3, 6):
            z_blocks[rb, i] = _dot(u[rb * blk:(rb + 1) * blk], win_ref[:, i * Z_GROUP:(i + 1) * Z_GROUP])

    def zpart(i):
        return jnp.concatenate([z_blocks[rb, i] for rb in range(n_blocks)], axis=0)

    a = zpart(4) * _sigmoid(zpart(5))
    convw = convw_ref[...]
    convb = convb_ref[...]
    if time_major_conv:
        for cc in range(N_LANE_SLABS):
            y_ref[cc] = a[:, cc * LANES:(cc + 1) * LANES]
        for t in range(tt):
            for cc in range(N_LANE_SLABS):
                xbuf_ref[TAIL + t, :, cc * LANES:(cc + 1) * LANES] = y_ref[cc, pl.ds(t, bb, stride=tt), :]
        for t in range(tt):
            acc = None
            for tap in range(CONV_WIDTH):
                term = xbuf_ref[t + tap] * convw[tap:tap + 1, :]
                acc = term if acc is None else acc + term
            acc = acc + convb
            for cc in range(N_LANE_SLABS):
                y_ref[cc, pl.ds(t, bb, stride=tt), :] = acc[:, cc * LANES:(cc + 1) * LANES]
    else:
        for b in range(bb):
            for cc in range(N_LANE_SLABS):
                xbuf_ref[b, cc, pl.ds(XROW0, tt), :] = a[b * tt:(b + 1) * tt, cc * LANES:(cc + 1) * LANES]

    def conv_rows(b, t0, nt):
        for cc in range(N_LANE_SLABS):
            ls = slice(cc * LANES, (cc + 1) * LANES)
            acc = None
            for tap in range(CONV_WIDTH):
                term = xbuf_ref[b, cc, pl.ds(TAIL_ROW0 + t0 + tap, nt), :] * convw[tap:tap + 1, ls]
                acc = term if acc is None else acc + term
            y_ref[b * tt + t0:b * tt + t0 + nt, ls] = acc + convb[:, ls]

    q_all = zpart(0).astype(_BF16)
    zf = zpart(1)
    v_all = zpart(2).astype(_BF16)
    sig = 1.0 / (1.0 + jnp.exp2(zf * NEG_LOG2_E))
    sneg = 1.0 - sig
    k_all = (one_minus_lb * sneg).astype(_BF16)
    logf = jnp.log2(sig + lb_floor * sneg)
    parts = _split3(logf)

    ti = lax.broadcasted_iota(jnp.int32, (chunk, chunk), 0)
    si = lax.broadcasted_iota(jnp.int32, (chunk, chunk), 1)
    ri = lax.broadcasted_iota(jnp.int32, (chunk, HEAD_DIM), 0)
    pair_masks = []
    second_half = []
    for m in levels:
        sh = int(np.log2(2 * m))
        same = (ti >> sh) == (si >> sh)
        pair_masks.append(same & ((ti & (2 * m - 1)) >= m) & ((si & (2 * m - 1)) < m))
        second_half.append((ri & (2 * m - 1)) >= m)
    diag_mask = ti == si
    amat = amat_ref[...]
    amat3 = jnp.concatenate([amat] * 3, axis=1)
    chunk_ids = [(b, c) for b in range(bb) for c in range(n_chunks)]
    heads = [slice(hh * HEAD_DIM, (hh + 1) * HEAD_DIM) for hh in range(N_HEADS)]

    odd_row = (lax.broadcasted_iota(jnp.int32, (chunk, D_A), 0) & 1) == 1
    e_all = {}
    for (b, c) in chunk_ids:
        rs = slice(b * tt + c * chunk, b * tt + (c + 1) * chunk)
        if chunk % BF16_ROWS == 0:
            args = _dot(amat3, jnp.concatenate([pt[rs] for pt in parts], axis=0))
        else:
            stacked_parts = jnp.concatenate([pt[rs].astype(_F32) for pt in parts], axis=0).astype(_BF16)
            args = _dot(amat3, stacked_parts)
        b_run = args[(nlev - 1) * chunk:nlev * chunk]
        lvl1 = jnp.where(odd_row, logf[rs], 0.0)
        to_end = b_run[chunk - 1:chunk, :] - b_run
        e_all[b, c] = jnp.exp2(jnp.concatenate([args[0:(nlev - 1) * chunk], lvl1, b_run, to_end], axis=0))

    scores = {}
    for (b, c) in chunk_ids:
        rs = slice(b * tt + c * chunk, b * tt + (c + 1) * chunk)
        for hh, hs in enumerate(heads):
            qh = q_all[rs, hs]
            kh = k_all[rs, hs]
            eh = e_all[b, c][:, hs]
            prods = []
            for lv in range(nlev):
                em = eh[lv * chunk:(lv + 1) * chunk].astype(_BF16)
                xm = jnp.where(second_half[lv], qh, kh) * em
                prods.append(_dot_nt(xm, xm))
            pm = jnp.where(diag_mask, _dot_nt(qh, kh), 0.0)
            for lv in range(nlev):
                pm = jnp.where(pair_masks[lv], prods[lv], pm)
            scores[b, c, hh] = pm.astype(_BF16)
        if not time_major_conv:
            conv_rows(b, c * chunk, chunk)

    for b in range(bb):
        st = [st_ref[b, hh] for hh in range(N_HEADS)]
        for c in range(n_chunks):
            rs = slice(b * tt + c * chunk, b * tt + (c + 1) * chunk)
            for hh, hs in enumerate(heads):
                eh = e_all[b, c][:, hs]
                e_q = eh[nlev * chunk:(nlev + 1) * chunk]
                e_k = eh[(nlev + 1) * chunk:(nlev + 2) * chunk]
                qt = q_all[rs, hs] * e_q.astype(_BF16)
                kt = k_all[rs, hs] * e_k.astype(_BF16)
                vh = v_all[rs, hs]
                o = _dot(scores[b, c, hh], vh) + _dot_nt(qt, st[hh].astype(_BF16))
                st[hh] = st[hh] * e_q[chunk - 1:chunk, :] + _dot_tn(vh, kt)
                oa_ref[rs, hs] = o
        for hh in range(N_HEADS):
            st_ref[b, hh] = st[hh]

    if time_major_conv:
        y = jnp.concatenate([y_ref[cc] for cc in range(N_LANE_SLABS)], axis=1)
    else:
        for b in range(bb):
            for cc in range(N_LANE_SLABS):
                new_tail = xbuf_ref[b, cc, pl.ds(TAIL_ROW0 + tt, TAIL), :]
                xbuf_ref[b, cc, pl.ds(TAIL_ROW0, TAIL), :] = new_tail
        y = y_ref[...]
    mu = jnp.mean(y, axis=-1, keepdims=True)
    yc = y - mu
    y = yc * lax.rsqrt(jnp.mean(yc * yc, axis=-1, keepdims=True) + LN_EPS) * lng_ref[...] + lnb_ref[...]
    y = _silu(y)
    y = _dot(y.astype(_BF16), wpw2_ref[...]) + bpw2_ref[...]
    gb = zpart(6)
    y = (y * _silu(gb)).astype(_BF16)

    ga = zpart(3)
    ghead = ghead_ref[...]
    o_parts = []
    for hh, hs in enumerate(heads):
        oh = oa_ref[:, hs]
        oh = oh * lax.rsqrt(jnp.mean(oh * oh, axis=-1, keepdims=True) + RMS_EPS) * ghead
        gah = ga[:, hs]
        o_parts.append((oh * _silu(gah)).astype(_BF16))
    o_a = jnp.concatenate(o_parts, axis=-1)

    out = _dot(jnp.concatenate([o_a, y], axis=-1), wout_ref[...])
    out = out * lax.rsqrt(jnp.mean(out * out, axis=-1, keepdims=True) + RMS_EPS) * gpost_ref[...]
    h_new = x + out
    if n_layers > 1:
        hall_ref[gi, j] = h_new
    ho_ref[...] = h_new.reshape(bb, tt, D_MODEL)

    if single_tile:
        store_state()
    else:
        pl.when(j == n_tiles - 1)(store_state)


def _mixer_layers(layer0, n_layers, h, s0, c0, params, *, bb, tt, chunk, state_layer0=0):
    (lb_logits, norm_pre, norm_post, w_in, head_norm, conv_w, conv_b, ln_g, ln_b,
     w_pw2, b_pw2, w_out) = params
    batch, seq, _ = h.shape
    assert batch % bb == 0 and seq % tt == 0 and tt % chunk == 0
    shared_state = s0.shape[1] == 1 and batch > 1
    assert not shared_state or bb == 1
    amat = jnp.asarray(_decay_arg_matrix(chunk), _BF16)
    rows = bb * tt
    n_groups, n_tiles = batch // bb, seq // tt
    time_major_conv = n_tiles == 1 and tt == SUBLANES and bb % SUBLANES == 0

    def state_idx(nd):
        if shared_state:
            return lambda l, i, j: (state_layer0 + l, 0) + (0,) * nd
        return lambda l, i, j: (state_layer0 + l, i) + (0,) * nd

    if time_major_conv:
        c0 = jnp.swapaxes(c0, 1, 2)
        tail_block = (None, TAIL, bb, D_B)
        tail_in_idx = lambda l, i, j: (state_layer0 + l, 0, i, 0)
        tail_out_idx = lambda l, i, j: (l, 0, i, 0)
        tail_out_shape = (n_layers, TAIL, batch, D_B)
    else:
        tail_block = (None, bb, TAIL, D_B)
        tail_in_idx = state_idx(2)
        tail_out_idx = lambda l, i, j: (l, i, 0, 0)
        tail_out_shape = (n_layers, batch, TAIL, D_B)

    def whole(arr):
        nd = arr.ndim
        return pl.BlockSpec(arr.shape, lambda l, i, j: (0,) * nd)

    def per_layer(arr):
        tail = arr.shape[1:]
        nz = len(tail)
        return pl.BlockSpec((None,) + tail, lambda l, i, j: (layer0 + l,) + (0,) * nz)

    state_block = (None, bb, N_HEADS, HEAD_DIM, HEAD_DIM)
    in_specs = [
        pl.BlockSpec((bb, tt, D_MODEL), lambda l, i, j: (i, j, 0)),
        pl.BlockSpec(state_block, state_idx(3)),
        pl.BlockSpec(tail_block, tail_in_idx),
        whole(amat),
        whole(lb_logits),
        per_layer(norm_pre), per_layer(norm_post), per_layer(w_in), per_layer(head_norm),
        per_layer(conv_w), per_layer(conv_b), per_layer(ln_g), per_layer(ln_b),
        per_layer(w_pw2), per_layer(b_pw2), per_layer(w_out),
    ]
    last = n_layers - 1

    def h_out_idx(l, i, j):
        if n_layers == 1:
            return (i, j, 0)
        return (jnp.where(l == last, i, 0), jnp.where(l == last, j, 0), 0)

    out_specs = [
        pl.BlockSpec((bb, tt, D_MODEL), h_out_idx),
        pl.BlockSpec(state_block, lambda l, i, j: (l, i, 0, 0, 0)),
        pl.BlockSpec(tail_block, tail_out_idx),
    ]
    out_shape = [
        jax.ShapeDtypeStruct((batch, seq, D_MODEL), h.dtype),
        jax.ShapeDtypeStruct((n_layers, batch, N_HEADS, HEAD_DIM, HEAD_DIM), h.dtype),
        jax.ShapeDtypeStruct(tail_out_shape, h.dtype),
    ]
    hall_shape = (n_groups, n_tiles, rows, D_MODEL) if n_layers > 1 else (1, 1, SUBLANES, LANES)
    scratch = [
        pltpu.VMEM((bb, N_HEADS, HEAD_DIM, HEAD_DIM), _F32),
        pltpu.VMEM((TAIL + tt, bb, D_B) if time_major_conv
                   else (bb, N_LANE_SLABS, XROW0 + tt, LANES), _F32),
        pltpu.VMEM((rows, D_A), _F32),
        pltpu.VMEM((N_LANE_SLABS, rows, LANES) if time_major_conv else (rows, D_B), _F32),
        pltpu.VMEM(hall_shape, _F32),
    ]
    h_out, s_out, c_out = pl.pallas_call(
        functools.partial(_layer_kernel, layer0, n_layers, bb, tt, chunk, n_tiles),
        grid=(n_layers, n_groups, n_tiles),
        in_specs=in_specs,
        out_specs=out_specs,
        out_shape=out_shape,
        scratch_shapes=scratch,
        compiler_params=pltpu.CompilerParams(
            dimension_semantics=("arbitrary", "arbitrary", "arbitrary"),
            vmem_limit_bytes=VMEM_LIMIT_BYTES),
    )(h, s0, c0, amat, lb_logits, norm_pre, norm_post, w_in, head_norm, conv_w, conv_b,
      ln_g, ln_b, w_pw2, b_pw2, w_out)
    if time_major_conv:
        c_out = jnp.swapaxes(c_out, 1, 2)
    return h_out, s_out, c_out


def kernel(x_prompt, x_sample, state_hgrn, state_conv, meta_tokens, norm_pre, norm_post, w_in,
           lb_logits, head_norm, conv_w, conv_b, conv_ln_g, conv_ln_b, w_pw2, b_pw2, w_out):
    depth = w_in.shape[0]
    row3 = lambda a: a.reshape(depth, 1, a.shape[-1])
    params = (lb_logits, row3(norm_pre), row3(norm_post), w_in.astype(_BF16), row3(head_norm),
              conv_w, row3(conv_b), row3(conv_ln_g), row3(conv_ln_b), w_pw2.astype(_BF16),
              row3(b_pw2), w_out.astype(_BF16))
    dtype = x_prompt.dtype

    n_dec = x_sample.shape[1]
    y_sample, s_sample, c_sample = _mixer_layers(
        0, depth, x_sample, state_hgrn, state_conv, params, bb=SAMPLE_ROWS, tt=n_dec, chunk=n_dec)

    hm = meta_tokens[None].astype(dtype)
    zero_s = jnp.zeros((depth, 1, N_HEADS, HEAD_DIM, HEAD_DIM), dtype)
    zero_c = jnp.zeros((depth, 1, TAIL, D_B), dtype)
    _, s_meta, c_meta = _mixer_layers(0, depth, hm, zero_s, zero_c, params, bb=1, tt=N_META,
                                      chunk=N_META)
    hp = x_prompt
    sp_l, cp_l = [], []
    for l in range(depth):
        hp, sp, cp = _mixer_layers(l, 1, hp, s_meta, c_meta, params, bb=1, tt=PROMPT_TILE,
                                   chunk=PROMPT_CHUNK, state_layer0=l)
        sp_l.append(sp)
        cp_l.append(cp)
    return (hp, y_sample, jnp.concatenate(sp_l), jnp.concatenate(cp_l), s_sample, c_sample)
```
